```python
import functools
import jax, jax.numpy as jnp
from jax import lax
import numpy as np

D_MODEL = 1024
BATCH = 8
SEQ = 2048
DEPTH = 4
DEC_BATCH = 128
DEC_SEQ = 8
PAST_LEN = 2048
PAGE_SIZE = 128

N_META = 16
BLOCK = 128
H_A = 8
DH_A = 64
D_A = H_A * DH_A
C_B = 512
CONV_W = 31
H_C = 4
DK_C = 128
DV_C = 256
D_QK_C = H_C * DK_C
D_V_C = H_C * DV_C
N_BRANCH = 3
D_FF = 4 * D_MODEL
D_IN = 3 * D_A + H_A + 2 * C_B + 2 * D_QK_C + 2 * D_V_C + N_BRANCH * D_MODEL
EPS = 1e-6
FORGET_BIAS = 3.0
ROPE_BASE = 10000.0

kernel_name = 'fox_conformer_retention_hybrid_step'


def _split_points():
    sizes = (D_A, D_A, D_A, H_A, 2 * C_B, D_QK_C, D_QK_C, D_V_C, D_V_C, N_BRANCH * D_MODEL)
    return [int(s) for s in np.cumsum(sizes)[:-1]]


def rmsnorm(x, g):
    xf = x.astype(jnp.float32)
    y = xf * lax.rsqrt(jnp.mean(xf * xf, axis=-1, keepdims=True) + EPS)
    return (y * g.astype(jnp.float32)).astype(x.dtype)


def layernorm(x, g, b):
    xf = x.astype(jnp.float32)
    mu = jnp.mean(xf, axis=-1, keepdims=True)
    var = jnp.mean(jnp.square(xf - mu), axis=-1, keepdims=True)
    return ((xf - mu) * lax.rsqrt(var + EPS) * g.astype(jnp.float32) + b.astype(jnp.float32)).astype(x.dtype)


def rotary(x, pos):
    half = x.shape[-1] // 2
    inv = ROPE_BASE ** (-jnp.arange(half, dtype=jnp.float32) / half)
    ang = pos.astype(jnp.float32)[:, None] * inv[None, :]
    cos = jnp.cos(ang)[None, :, None, :]
    sin = jnp.sin(ang)[None, :, None, :]
    xf = x.astype(jnp.float32)
    x1, x2 = xf[..., :half], xf[..., half:]
    return jnp.concatenate([x1 * cos - x2 * sin, x1 * sin + x2 * cos], axis=-1)


def retention_log_gamma():
    return jnp.log1p(-jnp.exp2(-5.0 - jnp.arange(H_C, dtype=jnp.float32)))


def retention_chunk(S, q, k, v, log_gamma):
    n = q.shape[1]
    j = jnp.arange(n, dtype=jnp.float32)
    rel = j[:, None] - j[None, :]
    decay = jnp.where(rel >= 0, jnp.exp(jnp.maximum(rel, 0.0)[None] * log_gamma[:, None, None]), 0.0)
    scores = jnp.einsum('bqhd,bkhd->bhqk', q, k) * decay[None]
    o = jnp.einsum('bhqk,bkhe->bqhe', scores, v)
    cross = jnp.exp((j + 1.0)[None, :] * log_gamma[:, None])
    o = o + jnp.einsum('bqhd,bhde->bqhe', q, S) * cross.T[None, :, :, None]
    k_scale = jnp.exp((n - 1.0 - j)[None, :] * log_gamma[:, None])
    S_new = jnp.exp(n * log_gamma)[None, :, None, None] * S + jnp.einsum('bkhd,bkhe->bhde', k * k_scale.T[None, :, :, None], v)
    return S_new, o


def ret_prompt(q, k, v, log_gamma):
    bn = q.shape[0]
    S0 = jnp.zeros((bn, H_C, DK_C, DV_C), jnp.float32)
    vf = v.astype(jnp.float32)
    S, o_meta = retention_chunk(S0, q[:, :N_META], k[:, :N_META], vf[:, :N_META], log_gamma)

    def to_blocks(t):
        return jnp.moveaxis(t[:, N_META:].reshape(bn, -1, BLOCK, t.shape[2], t.shape[3]), 1, 0)

    S, o_blk = lax.scan(lambda s, c: retention_chunk(s, c[0], c[1], c[2], log_gamma), S,
                        (to_blocks(q), to_blocks(k), to_blocks(vf)))
    o_real = jnp.moveaxis(o_blk, 0, 1).reshape(bn, -1, H_C, DV_C)
    return jnp.concatenate([o_meta, o_real], axis=1), S


def ret_sample(q, k, v, S_prev, log_gamma):
    S, o = retention_chunk(S_prev.astype(jnp.float32), q, k, v.astype(jnp.float32), log_gamma)
    return o, S


def fox_attend(q, k, v, c_q, c_k, qpos, kpos):
    s = jnp.einsum('bqhd,bkhd->bhqk', q, k).astype(jnp.float32) * (DH_A ** -0.5)
    bias = jnp.swapaxes(c_q, 1, 2)[..., :, None] - jnp.swapaxes(c_k, 1, 2)[..., None, :]
    s = jnp.where(qpos[:, None] >= kpos[None, :], s + bias, -jnp.inf)
    p = jax.nn.softmax(s, axis=-1)
    return jnp.einsum('bhqk,bkhd->bqhd', p.astype(v.dtype), v)


def fox_prompt(q, k, v, logf):
    bn, L = q.shape[0], q.shape[1]
    c = jnp.cumsum(logf.astype(jnp.float32), axis=1)
    pos = jnp.arange(L)
    o_meta = fox_attend(q[:, :N_META], k[:, :N_META], v[:, :N_META], c[:, :N_META], c[:, :N_META],
                        pos[:N_META], pos[:N_META])
    n_blk = (L - N_META) // BLOCK
    q_blk = jnp.moveaxis(q[:, N_META:].reshape(bn, n_blk, BLOCK, H_A, DH_A), 1, 0)
    c_blk = jnp.moveaxis(c[:, N_META:].reshape(bn, n_blk, BLOCK, H_A), 1, 0)
    p_blk = pos[N_META:].reshape(n_blk, BLOCK)
    o_blk = lax.map(lambda a: fox_attend(a[0], k, v, a[1], c, a[2], pos), (q_blk, c_blk, p_blk))
    o_real = jnp.moveaxis(o_blk, 0, 1).reshape(bn, L - N_META, H_A, DH_A)
    return jnp.concatenate([o_meta, o_real], axis=1)


def fox_sample(q, k, v, logf, k_past, v_past, logf_past):
    c_past = jnp.cumsum(logf_past.astype(jnp.float32), axis=1)
    c_new = c_past[:, -1:] + jnp.cumsum(logf.astype(jnp.float32), axis=1)
    k_all = jnp.concatenate([k_past.astype(k.dtype), k], axis=1)
    v_all = jnp.concatenate([v_past.astype(v.dtype), v], axis=1)
    c_all = jnp.concatenate([c_past, c_new], axis=1)
    P, n = c_past.shape[1], q.shape[1]
    return fox_attend(q, k_all, v_all, c_new, c_all, P + jnp.arange(n), jnp.arange(P + n))


def mixer(h, pos, attend_fn, ret_fn, conv_buf, w_in, b_forget, w_dw, b_dw, ln_conv_g, ln_conv_b,
          w_pw_out, w_attn_out, gn_ret_g, w_ret_out, w_o):
    bn, L = h.shape[0], h.shape[1]
    z = jnp.einsum('bld,de->ble', h, w_in)
    q_a, k_a, v_a, f_a, glu_in, q_c, k_c, v_c, g_c, gate_logits = jnp.split(z, _split_points(), axis=-1)
    heads = lambda t, n_h: t.reshape(bn, L, n_h, -1)

    q_a, k_a, v_a = heads(q_a, H_A), heads(k_a, H_A), heads(v_a, H_A)
    logf = jax.nn.log_sigmoid(f_a.astype(jnp.float32) + b_forget.astype(jnp.float32))
    o_a = attend_fn(q_a, k_a, v_a, logf).reshape(bn, L, D_A)
    y_a = jnp.einsum('ble,ed->bld', o_a, w_attn_out)

    a_glu, b_glu = jnp.split(glu_in, 2, axis=-1)
    u = a_glu * jax.nn.sigmoid(b_glu)
    u_ext = jnp.concatenate([conv_buf.astype(u.dtype), u], axis=1)
    dw = lax.conv_general_dilated(u_ext, w_dw[:, None, :].astype(u.dtype), window_strides=(1,), padding='VALID',
                                  dimension_numbers=('NWC', 'WIO', 'NWC'), feature_group_count=C_B) + b_dw
    new_buf = u_ext[:, -(CONV_W - 1):]
    y_b = jnp.einsum('blc,cd->bld', jax.nn.silu(layernorm(dw, ln_conv_g, ln_conv_b)), w_pw_out)

    qr = rotary(heads(q_c, H_C), pos)
    kr = rotary(heads(k_c, H_C), pos) * (DK_C ** -0.5)
    o_c, S = ret_fn(qr, kr, heads(v_c, H_C))
    mu = jnp.mean(o_c, axis=-1, keepdims=True)
    var = jnp.mean(jnp.square(o_c - mu), axis=-1, keepdims=True)
    o_c = ((o_c - mu) * lax.rsqrt(var + EPS)).reshape(bn, L, D_V_C) * gn_ret_g.astype(jnp.float32)
    y_c = jnp.einsum('ble,ed->bld', (jax.nn.silu(g_c.astype(jnp.float32)) * o_c).astype(h.dtype), w_ret_out)

    g = jax.nn.sigmoid(gate_logits).reshape(bn, L, N_BRANCH, D_MODEL)
    merged = g[:, :, 0] * y_a + g[:, :, 1] * y_b + g[:, :, 2] * y_c
    out = jnp.einsum('bld,de->ble', merged, w_o)
    return out, k_a, v_a, logf.astype(h.dtype), new_buf, S.astype(h.dtype)


def trunk_layer(x, pos, attend_fn, ret_fn, conv_buf, norms, mix_w, w_ff_up, w_ff_down):
    nm_pre, nm_post, nf_pre, nf_post = norms
    mix, k_a, v_a, logf, new_buf, S = mixer(rmsnorm(x, nm_pre), pos, attend_fn, ret_fn, conv_buf, *mix_w)
    x = x + rmsnorm(mix, nm_post)
    h = rmsnorm(x, nf_pre)
    ff = jnp.einsum('blf,fd->bld', jnp.square(jax.nn.relu(jnp.einsum('bld,df->blf', h, w_ff_up))), w_ff_down)
    x = x + rmsnorm(ff, nf_post)
    return x, k_a, v_a, logf, new_buf, S


def setup_inputs(seed: int = 0) -> dict:
    key = jax.random.key(seed)
    ks = jax.random.split(key, 32)
    n_pages = PAST_LEN // PAGE_SIZE
    n_pool = (DEC_BATCH * n_pages * 5) // 4
    nrm = lambda k, shape, scale: scale * jax.random.normal(k, shape, jnp.float32)
    page_table = jax.random.permutation(ks[7], n_pool)[:DEC_BATCH * n_pages].reshape(DEC_BATCH, n_pages).astype(jnp.int32)
    return {
        'x_prompt': nrm(ks[0], (BATCH, SEQ, D_MODEL), 1.0),
        'x_sample': nrm(ks[1], (DEC_BATCH, DEC_SEQ, D_MODEL), 1.0),
        'cache_k': nrm(ks[2], (DEPTH, n_pool, PAGE_SIZE, H_A, DH_A), 1.0),
        'cache_v': nrm(ks[3], (DEPTH, n_pool, PAGE_SIZE, H_A, DH_A), 1.0),
        'cache_logf': jax.nn.log_sigmoid(FORGET_BIAS + jax.random.normal(ks[4], (DEPTH, n_pool, PAGE_SIZE, H_A), jnp.float32)),
        'state_conv': nrm(ks[5], (DEPTH, DEC_BATCH, CONV_W - 1, C_B), 0.5),
        'state_ret': nrm(ks[6], (DEPTH, DEC_BATCH, H_C, DK_C, DV_C), 0.5),
        'page_table': page_table,
        'meta_tokens': nrm(ks[8], (N_META, D_MODEL), 1.0),
        'norm_mix_pre': 1.0 + nrm(ks[9], (DEPTH, D_MODEL), 0.02),
        'norm_mix_post': 1.0 + nrm(ks[10], (DEPTH, D_MODEL), 0.02),
        'norm_ffn_pre': 1.0 + nrm(ks[11], (DEPTH, D_MODEL), 0.02),
        'norm_ffn_post': 1.0 + nrm(ks[12], (DEPTH, D_MODEL), 0.02),
        'w_in': nrm(ks[13], (DEPTH, D_MODEL, D_IN), D_MODEL ** -0.5),
        'b_forget': FORGET_BIAS + nrm(ks[14], (DEPTH, H_A), 0.1),
        'w_dw': nrm(ks[15], (DEPTH, CONV_W, C_B), CONV_W ** -0.5),
        'b_dw': nrm(ks[16], (DEPTH, C_B), 0.02),
        'ln_conv_g': 1.0 + nrm(ks[17], (DEPTH, C_B), 0.02),
        'ln_conv_b': nrm(ks[18], (DEPTH, C_B), 0.02),
        'w_pw_out': nrm(ks[19], (DEPTH, C_B, D_MODEL), C_B ** -0.5),
        'w_attn_out': nrm(ks[20], (DEPTH, D_A, D_MODEL), D_A ** -0.5),
        'gn_ret_g': 1.0 + nrm(ks[21], (DEPTH, D_V_C), 0.02),
        'w_ret_out': nrm(ks[22], (DEPTH, D_V_C, D_MODEL), D_V_C ** -0.5),
        'w_o': nrm(ks[23], (DEPTH, D_MODEL, D_MODEL), D_MODEL ** -0.5),
        'w_ff_up': nrm(ks[24], (DEPTH, D_MODEL, D_FF), D_MODEL ** -0.5),
        'w_ff_down': nrm(ks[25], (DEPTH, D_FF, D_MODEL), D_FF ** -0.5),
    }


def reference(x_prompt, x_sample, cache_k, cache_v, cache_logf, state_conv, state_ret, page_table,
              meta_tokens, norm_mix_pre, norm_mix_post, norm_ffn_pre, norm_ffn_post,
              w_in, b_forget, w_dw, b_dw, ln_conv_g, ln_conv_b, w_pw_out, w_attn_out,
              gn_ret_g, w_ret_out, w_o, w_ff_up, w_ff_down):
    log_gamma = retention_log_gamma()
    n_dec = x_sample.shape[0]
    past_len = page_table.shape[1] * cache_k.shape[2]
    bp = x_prompt.shape[0]
    xp = jnp.concatenate([jnp.broadcast_to(meta_tokens[None].astype(x_prompt.dtype), (bp, N_META, D_MODEL)), x_prompt], axis=1)
    xs = x_sample
    pos_p = jnp.arange(xp.shape[1])
    pos_s = past_len + jnp.arange(xs.shape[1])
    buf0 = jnp.zeros((bp, CONV_W - 1, C_B), xp.dtype)
    ret_p = functools.partial(ret_prompt, log_gamma=log_gamma)
    pk, pv, plf, pcv, prt = [], [], [], [], []
    sk, sv, slf, scv, srt = [], [], [], [], []
    for l in range(DEPTH):
        norms = (norm_mix_pre[l], norm_mix_post[l], norm_ffn_pre[l], norm_ffn_post[l])
        mix_w = (w_in[l], b_forget[l], w_dw[l], b_dw[l], ln_conv_g[l], ln_conv_b[l], w_pw_out[l],
                 w_attn_out[l], gn_ret_g[l], w_ret_out[l], w_o[l])
        xp, k_a, v_a, lf, cb, S = trunk_layer(xp, pos_p, fox_prompt, ret_p, buf0, norms, mix_w, w_ff_up[l], w_ff_down[l])
        pk.append(k_a); pv.append(v_a); plf.append(lf); pcv.append(cb); prt.append(S)
        k_past = cache_k[l, page_table].reshape(n_dec, -1, H_A, DH_A)
        v_past = cache_v[l, page_table].reshape(n_dec, -1, H_A, DH_A)
        lf_past = cache_logf[l, page_table].reshape(n_dec, -1, H_A)
        attend_s = functools.partial(fox_sample, k_past=k_past, v_past=v_past, logf_past=lf_past)
        ret_s = functools.partial(ret_sample, S_prev=state_ret[l], log_gamma=log_gamma)
        xs, k_a, v_a, lf, cb, S = trunk_layer(xs, pos_s, attend_s, ret_s, state_conv[l], norms, mix_w, w_ff_up[l], w_ff_down[l])
        sk.append(k_a); sv.append(v_a); slf.append(lf); scv.append(cb); srt.append(S)
    y_prompt = xp[:, N_META:]
    return (y_prompt, xs,
            jnp.stack(pk), jnp.stack(pv), jnp.stack(plf), jnp.stack(pcv), jnp.stack(prt),
            jnp.stack(sk), jnp.stack(sv), jnp.stack(slf), jnp.stack(scv), jnp.stack(srt))
```

```python
import functools

import numpy as np
import jax
import jax.numpy as jnp
from jax import lax
from jax.experimental import pallas as pl
from jax.experimental.pallas import tpu as pltpu

D_MODEL = 1024
N_META = 16
H_A = 8
DH_A = 64
D_A = H_A * DH_A
C_B = 512
CONV_W = 31
H_C = 4
DK_C = 128
DV_C = 256
D_QK_C = H_C * DK_C
D_V_C = H_C * DV_C
N_BRANCH = 3
D_FF = 4 * D_MODEL
EPS = 1e-6
ROPE_BASE = 10000.0

LANES = 128
SUBLANES = 8
ATT_BLK = 256
RET_BLK = 128
NEG = -1e30
VMEM_LIMIT = 56 * 1024 * 1024

BF16 = jnp.bfloat16
F32 = jnp.float32

_LOG_GAMMA = tuple(float(np.log1p(-(2.0 ** (-5.0 - h)))) for h in range(H_C))


def _cparams(n_axes):
    return pltpu.CompilerParams(dimension_semantics=("arbitrary",) * n_axes,
                                vmem_limit_bytes=VMEM_LIMIT)


def _row_tile(total, target, must_divide=None):
    best = None
    for t in range(SUBLANES, min(total, target) + 1, SUBLANES):
        if total % t == 0 and (must_divide is None or must_divide % t == 0):
            best = t
    assert best is not None, (total, target, must_divide)
    return best


def _resident(shape, index_map):
    return pl.BlockSpec(shape, index_map, pipeline_mode=pl.Buffered(1))


def _dot(a, b):
    return jnp.dot(a, b, preferred_element_type=F32)


def _dot_nt(a, b):
    return lax.dot_general(a, b, (((1,), (1,)), ((), ())), preferred_element_type=F32)


def _split3(x):
    hi = x.astype(BF16)
    r1 = x - hi.astype(F32)
    mid = r1.astype(BF16)
    r2 = r1 - mid.astype(F32)
    return hi, mid, r2.astype(BF16)


def _rms(x, g):
    return x * lax.rsqrt(jnp.mean(x * x, axis=-1, keepdims=True) + EPS) * g


def _log_sigmoid(z):
    return jnp.minimum(z, 0.0) - jnp.log1p(jnp.exp(-jnp.abs(z)))


def _sigmoid(z):
    return 1.0 / (1.0 + jnp.exp(-z))


def _inproj_kernel(x_ref, g_ref, wa_ref, wf_ref, bf_ref, wglu_ref, wqk_ref, wv_ref, cos_ref, sin_ref,
                   qa_ref, ka_ref, va_ref, lf_ref, u_ref, qr_ref, kr_ref, vc_ref):
    h = _rms(x_ref[...], g_ref[...]).astype(BF16)
    za = _dot(h, wa_ref[...])
    qa_ref[...] = (za[:, :D_A] * (DH_A ** -0.5)).astype(BF16)
    ka_ref[...] = za[:, D_A:2 * D_A]
    va_ref[...] = za[:, 2 * D_A:]
    lf_ref[...] = _log_sigmoid(_dot(h, wf_ref[...]) + bf_ref[...])
    zg = _dot(h, wglu_ref[...])
    u_ref[...] = zg[:, :C_B] * _sigmoid(zg[:, C_B:])
    zqk = _dot(h, wqk_ref[...])
    cos = cos_ref[...]
    sin = sin_ref[...]
    for j in range(2 * H_C):
        xh = zqk[:, j * DK_C:(j + 1) * DK_C]
        r = xh * cos + pltpu.roll(xh, DK_C // 2, 1) * sin
        if j < H_C:
            qr_ref[:, j * DK_C:(j + 1) * DK_C] = r.astype(BF16)
        else:
            kr_ref[:, (j - H_C) * DK_C:(j - H_C + 1) * DK_C] = r * (DK_C ** -0.5)
    vc_ref[...] = _dot(h, wv_ref[...]).astype(BF16)


def _inproj(x, l, gpre, wa, wf, bfp, wglu, wqk, wv, cos_t, sin_t, tm):
    t = x.shape[0]
    n_tab = cos_t.shape[0] // tm
    row = lambda n: pl.BlockSpec((tm, n), lambda i: (i, 0))
    wspec = lambda n: _resident((None, D_MODEL, n), lambda i: (l, 0, 0))
    tab = pl.BlockSpec((tm, LANES), lambda i: (i % n_tab, 0))
    outs = [(D_A, BF16), (D_A, F32), (D_A, F32), (LANES, F32), (C_B, F32),
            (D_QK_C, BF16), (D_QK_C, F32), (D_V_C, BF16)]
    return pl.pallas_call(
        _inproj_kernel,
        grid=(t // tm,),
        in_specs=[row(D_MODEL), _resident((None, 1, D_MODEL), lambda i: (l, 0, 0)),
                  wspec(3 * D_A), wspec(LANES), _resident((None, 1, LANES), lambda i: (l, 0, 0)),
                  wspec(2 * C_B), wspec(2 * D_QK_C), wspec(D_V_C), tab, tab],
        out_specs=[row(n) for n, _ in outs],
        out_shape=[jax.ShapeDtypeStruct((t, n), d) for n, d in outs],
        compiler_params=_cparams(1),
        name="inproj",
    )(x, gpre, wa, wf, bfp, wglu, wqk, wv, cos_t, sin_t)


def _fox_prompt_kernel(q_ref, k_ref, v_ref, lf_ref, o_ref, c_scr, qaug, kaug, vsep, *, seq):
    p = pl.program_id(1)
    n_full = seq // LANES
    tail = seq - n_full * LANES
    n_blk = (seq - N_META) // ATT_BLK

    @pl.when(p == 0)
    def _():
        r = lax.broadcasted_iota(jnp.int32, (LANES, LANES), 0)
        c = lax.broadcasted_iota(jnp.int32, (LANES, LANES), 1)
        tri = jnp.where(c <= r, 1.0, 0.0).astype(BF16)
        carry = jnp.zeros((1, LANES), F32)
        for b in range(n_full + (1 if tail else 0)):
            nb = LANES if b < n_full else tail
            hi, mid, lo = _split3(lf_ref[b * LANES:b * LANES + nb, :])
            tb = tri[:nb, :nb]
            cb = _dot(tb, hi) + _dot(tb, mid) + _dot(tb, lo) + carry
            c_scr[b * LANES:b * LANES + nb, :] = cb
            carry = cb[nb - 1:nb, :]

    lane = lax.broadcasted_iota(jnp.int32, (seq, LANES), 1)
    sr = lax.broadcasted_iota(jnp.int32, (LANES, LANES), 0)
    sc = lax.broadcasted_iota(jnp.int32, (LANES, LANES), 1)
    c_hi, c_mid, c_lo = _split3(c_scr[...])
    for hh in range(2):
        h = 2 * p + hh
        own = (lane >= hh * DH_A) & (lane < (hh + 1) * DH_A)
        x0 = (1 - hh) * DH_A
        aug = jnp.zeros((seq, LANES), F32)
        for j, piece in enumerate((c_hi, c_mid, c_lo)):
            sel = (jnp.where((sr == h) & (sc == x0 + j), 1.0, 0.0)
                   - jnp.where((sr == h) & (sc == x0 + 3 + j), 1.0, 0.0)).astype(BF16)
            aug = aug + _dot(piece, sel)
        q_extra = jnp.where(lane < x0 + 3, aug, jnp.where(lane < x0 + 6, 1.0, 0.0))
        k_extra = jnp.where(lane < x0 + 3, 1.0, jnp.where(lane < x0 + 6, aug, 0.0))
        in_x = (lane >= x0) & (lane < x0 + 6)
        qaug[hh] = jnp.where(own, q_ref[...].astype(F32), jnp.where(in_x, q_extra, 0.0)).astype(BF16)
        kaug[hh] = jnp.where(own, k_ref[...], jnp.where(in_x, k_extra, 0.0)).astype(BF16)
        vsep[hh] = jnp.where(own, v_ref[...], 0.0).astype(BF16)

    def step(hh, qa, start, size, carry, causal):
        m, l, acc = carry
        s = _dot_nt(qa, kaug[hh, pl.ds(start, size), :])
        if causal:
            rr = lax.broadcasted_iota(jnp.int32, s.shape, 0)
            cc = lax.broadcasted_iota(jnp.int32, s.shape, 1)
            s = jnp.where(rr >= cc, s, NEG)
        m_new = jnp.maximum(m, jnp.max(s, axis=1, keepdims=True))
        alpha = jnp.exp(m - m_new)
        pr = jnp.exp(s - m_new)
        l = alpha * l + jnp.sum(pr, axis=1, keepdims=True)
        acc = alpha * acc + _dot(pr.astype(BF16), vsep[hh, pl.ds(start, size), :])
        return m_new, l, acc

    def init(n):
        return (jnp.full((n, 1), NEG, F32), jnp.zeros((n, 1), F32), jnp.zeros((n, LANES), F32))

    o_meta = jnp.zeros((N_META, LANES), F32)
    for hh in range(2):
        m, l, acc = step(hh, qaug[hh, 0:N_META, :], 0, N_META, init(N_META), True)
        o_meta = o_meta + acc / l
    o_ref[0:N_META, :] = o_meta.astype(o_ref.dtype)

    def q_block(i, _):
        q0 = pl.multiple_of(N_META + i * ATT_BLK, N_META)
        out = jnp.zeros((ATT_BLK, LANES), F32)
        for hh in range(2):
            qa = qaug[hh, pl.ds(q0, ATT_BLK), :]
            carry = step(hh, qa, 0, N_META, init(ATT_BLK), False)

            def kv_block(j, c, hh=hh, qa=qa):
                k0 = pl.multiple_of(N_META + j * ATT_BLK, N_META)
                return step(hh, qa, k0, ATT_BLK, c, False)

            carry = lax.fori_loop(0, i, kv_block, carry)
            m, l, acc = step(hh, qa, q0, ATT_BLK, carry, True)
            out = out + acc / l
        o_ref[pl.ds(q0, ATT_BLK), :] = out.astype(o_ref.dtype)
        return 0

    lax.fori_loop(0, n_blk, q_block, 0)


def _fox_prompt(qa, ka, va, lf):
    b, seq, _ = qa.shape
    assert (seq - N_META) % ATT_BLK == 0
    pair = pl.BlockSpec((None, seq, LANES), lambda i, p: (i, 0, p))
    return pl.pallas_call(
        functools.partial(_fox_prompt_kernel, seq=seq),
        grid=(b, H_A // 2),
        in_specs=[pair, pair, pair, pl.BlockSpec((None, seq, LANES), lambda i, p: (i, 0, 0))],
        out_specs=pair,
        out_shape=jax.ShapeDtypeStruct((b, seq, D_A), BF16),
        scratch_shapes=[pltpu.VMEM((seq, LANES), F32), pltpu.VMEM((2, seq, LANES), BF16),
                        pltpu.VMEM((2, seq, LANES), BF16), pltpu.VMEM((2, seq, LANES), BF16)],
        compiler_params=_cparams(2),
        name="fox_prompt",
    )(qa, ka, va, lf)


def _pool_cumsum_kernel(lf_ref, t_ref, cin_ref, tot_ref):
    hi, mid, lo = _split3(lf_ref[...])
    t = t_ref[...]
    res = _dot(hi, t) + _dot(mid, t) + _dot(lo, t)
    cin_ref[...] = res[:, :LANES]
    tot_ref[...] = res[:, LANES:]


def _pool_cumsum(lf2d, tmat):
    rows = lf2d.shape[0]
    tm = _row_tile(rows, 2048)
    spec = pl.BlockSpec((tm, LANES), lambda i: (i, 0))
    return pl.pallas_call(
        _pool_cumsum_kernel,
        grid=(rows // tm,),
        in_specs=[spec, _resident((LANES, 2 * LANES), lambda i: (0, 0))],
        out_specs=[spec, spec],
        out_shape=[jax.ShapeDtypeStruct((rows, LANES), F32)] * 2,
        compiler_params=_cparams(1),
        name="pool_cumsum",
    )(lf2d, tmat)


def _fox_sample_kernel(pt_ref, q_ref, k_ref, v_ref, cin_ref, rtot_ref, kn_ref, vn_ref, lfn_ref, t_ref,
                       o_ref, m_scr, l_scr, acc_scr, carry_scr, *, n_pages, page):
    del pt_ref
    p = pl.program_id(1)
    nq = q_ref.shape[0]
    rows_per_chunk = LANES // H_A
    n_chunks = page // rows_per_chunk

    @pl.when(p == 0)
    def _():
        m_scr[...] = jnp.full(m_scr.shape, NEG, F32)
        l_scr[...] = jnp.zeros(l_scr.shape, F32)
        acc_scr[...] = jnp.zeros(acc_scr.shape, F32)
        carry_scr[...] = jnp.zeros(carry_scr.shape, F32)

    q = q_ref[...]
    n_tok = nq // H_A
    rr = lax.broadcasted_iota(jnp.int32, (nq, LANES), 0)
    cc = lax.broadcasted_iota(jnp.int32, (nq, LANES), 1)
    head_ok = (rr // n_tok) == (cc % H_A)

    rtot = rtot_ref[...]
    sub = lax.broadcasted_iota(jnp.int32, rtot.shape, 0)
    inc = rtot
    for sh in (1, 2, 4):
        inc = inc + jnp.where(sub >= sh, pltpu.roll(inc, sh, 0), 0.0)
    c_page = cin_ref[...] + (inc - rtot) + carry_scr[...]

    def update(s_list, v_list):
        m_old = m_scr[...]
        m_new = m_old
        for s in s_list:
            m_new = jnp.maximum(m_new, jnp.max(s, axis=1, keepdims=True))
        alpha = jnp.exp(m_old - m_new)
        l = alpha * l_scr[...]
        acc = alpha * acc_scr[...]
        for s, v in zip(s_list, v_list):
            pr = jnp.exp(s - m_new)
            l = l + jnp.sum(pr, axis=1, keepdims=True)
            acc = acc + _dot(pr.astype(BF16), v)
        m_scr[...] = m_new
        l_scr[...] = l
        acc_scr[...] = acc

    s_list, v_list = [], []
    for a in range(n_chunks):
        ka = k_ref[a * rows_per_chunk:(a + 1) * rows_per_chunk].reshape(LANES, DH_A).astype(BF16)
        va = v_ref[a * rows_per_chunk:(a + 1) * rows_per_chunk].reshape(LANES, DH_A).astype(BF16)
        s = _dot_nt(q, ka) - c_page[a:a + 1, :]
        s_list.append(jnp.where(head_ok, s, NEG))
        v_list.append(va)
    update(s_list, v_list)
    carry_scr[...] = carry_scr[...] + inc[SUBLANES - 1:SUBLANES, :]

    @pl.when(p == n_pages - 1)
    def _():
        hi, mid, lo = _split3(jnp.broadcast_to(lfn_ref[...], (SUBLANES, LANES)))
        t = t_ref[...]
        c_new = (_dot(hi, t) + _dot(mid, t) + _dot(lo, t))[0:1, :] + carry_scr[...]
        kn = kn_ref[...].astype(BF16)
        sn = _dot_nt(q, kn) - c_new[:, :nq]
        r2 = lax.broadcasted_iota(jnp.int32, (nq, nq), 0)
        c2 = lax.broadcasted_iota(jnp.int32, (nq, nq), 1)
        ok = ((r2 // n_tok) == (c2 % H_A)) & ((c2 // H_A) <= (r2 % n_tok))
        update([jnp.where(ok, sn, NEG)], [vn_ref[...].astype(BF16)])
        o_ref[...] = (acc_scr[...] / l_scr[...]).astype(o_ref.dtype)


def _fox_sample(l, page_flat, q_hq, cache_k, cache_v, cin, rtot, k_new, v_new, lf_new, tmat, n_pages):
    n_seq, nq, _ = q_hq.shape
    page = cache_k.shape[2]
    rows = page * H_A // LANES
    kv_spec = pl.BlockSpec((None, None, page, H_A, DH_A), lambda s, p, pt: (l, pt[s * n_pages + p], 0, 0, 0))
    c_spec = pl.BlockSpec((None, None, rows, LANES), lambda s, p, pt: (l, pt[s * n_pages + p], 0, 0))
    seq_spec = lambda a, b: pl.BlockSpec((None, a, b), lambda s, p, pt: (s, 0, 0))
    grid_spec = pltpu.PrefetchScalarGridSpec(
        num_scalar_prefetch=1,
        grid=(n_seq, n_pages),
        in_specs=[seq_spec(nq, DH_A), kv_spec, kv_spec, c_spec, c_spec,
                  seq_spec(nq, DH_A), seq_spec(nq, DH_A), seq_spec(1, LANES),
                  pl.BlockSpec((LANES, LANES), lambda s, p, pt: (0, 0))],
        out_specs=seq_spec(nq, DH_A),
        scratch_shapes=[pltpu.VMEM((nq, 1), F32), pltpu.VMEM((nq, 1), F32),
                        pltpu.VMEM((nq, DH_A), F32), pltpu.VMEM((1, LANES), F32)],
    )
    return pl.pallas_call(
        functools.partial(_fox_sample_kernel, n_pages=n_pages, page=page),
        grid_spec=grid_spec,
        out_shape=jax.ShapeDtypeStruct((n_seq, nq, DH_A), BF16),
        compiler_params=_cparams(2),
        name="fox_sample",
    )(page_flat, q_hq, cache_k, cache_v, cin, rtot, k_new, v_new, lf_new, tmat)


CONV_PAD = 32


def _conv_kernel(u_ref, buf_ref, w_ref, b_ref, g_ref, beta_ref, act_ref, nbuf_ref, ext, dw, *, nb, seq, tr):
    n_tiles = -(-seq // tr)
    keep = CONV_W - 1
    for s in range(nb):
        ext[s, 0:CONV_PAD - keep, :] = jnp.zeros((CONV_PAD - keep, C_B), F32)
        ext[s, CONV_PAD - keep:CONV_PAD, :] = buf_ref[s]
        ext[s, CONV_PAD:CONV_PAD + seq, :] = u_ref[s]
        if n_tiles * tr > seq:
            ext[s, CONV_PAD + seq:CONV_PAD + n_tiles * tr, :] = jnp.zeros((n_tiles * tr - seq, C_B), F32)
        nbuf_ref[s] = ext[s, CONV_PAD + seq - keep:CONV_PAD + seq, :]

        def tile(i, _, s=s):
            r0 = pl.multiple_of(i * tr, SUBLANES)
            for c in range(C_B // LANES):
                win = ext[s, pl.ds(r0, tr + CONV_PAD), c * LANES:(c + 1) * LANES]
                acc = jnp.broadcast_to(b_ref[:, c * LANES:(c + 1) * LANES], (tr, LANES))
                for j in range(CONV_W):
                    off = CONV_PAD - keep + j
                    acc = acc + win[off:off + tr, :] * w_ref[j:j + 1, c * LANES:(c + 1) * LANES]
                dw[pl.ds(r0, tr), c * LANES:(c + 1) * LANES] = acc
            return 0

        lax.fori_loop(0, n_tiles, tile, 0)

        def norm(r0, n, s=s):
            x = dw[pl.ds(r0, n), :]
            mu = jnp.mean(x, axis=-1, keepdims=True)
            xc = x - mu
            var = jnp.mean(xc * xc, axis=-1, keepdims=True)
            y = xc * lax.rsqrt(var + EPS) * g_ref[...] + beta_ref[...]
            act_ref[s, pl.ds(r0, n), :] = (y * _sigmoid(y)).astype(act_ref.dtype)

        n_full = seq // tr

        def norm_tile(i, _):
            norm(pl.multiple_of(i * tr, SUBLANES), tr)
            return 0

        lax.fori_loop(0, n_full, norm_tile, 0)
        if seq > n_full * tr:
            norm(n_full * tr, seq - n_full * tr)


def _conv(u, buf, l, w_dw, b_dw, ln_g, ln_b, nb):
    bsz, seq, _ = u.shape
    tr = LANES if seq >= LANES else seq
    n_tiles = -(-seq // tr)
    keep = CONV_W - 1
    par = lambda r: pl.BlockSpec((None, r, C_B), lambda i: (l, 0, 0))
    if buf.ndim == 4:
        buf_spec = pl.BlockSpec((None, nb, keep, C_B), lambda i: (l, i, 0, 0))
    else:
        buf_spec = pl.BlockSpec((nb, keep, C_B), lambda i: (i, 0, 0))
    return pl.pallas_call(
        functools.partial(_conv_kernel, nb=nb, seq=seq, tr=tr),
        grid=(bsz // nb,),
        in_specs=[pl.BlockSpec((nb, seq, C_B), lambda i: (i, 0, 0)), buf_spec,
                  par(CONV_PAD), par(1), par(1), par(1)],
        out_specs=[pl.BlockSpec((nb, seq, C_B), lambda i: (i, 0, 0)),
                   pl.BlockSpec((nb, keep, C_B), lambda i: (i, 0, 0))],
        out_shape=[jax.ShapeDtypeStruct((bsz, seq, C_B), BF16),
                   jax.ShapeDtypeStruct((bsz, keep, C_B), F32)],
        scratch_shapes=[pltpu.VMEM((nb, CONV_PAD + n_tiles * tr, C_B), F32),
                        pltpu.VMEM((n_tiles * tr, C_B), F32)],
        compiler_params=_cparams(1),
        name="conv",
    )(u, buf, w_dw, b_dw, ln_g, ln_b)


def _ret_chunk(s, q, k, v, n, lg):
    ii = lax.broadcasted_iota(jnp.int32, (n, n), 0)
    jj = lax.broadcasted_iota(jnp.int32, (n, n), 1)
    rel = (ii - jj).astype(F32)
    decay = jnp.where(rel >= 0, jnp.exp(jnp.maximum(rel, 0.0) * lg), 0.0)
    scores = _dot_nt(q, k.astype(BF16)) * decay
    o = _dot(scores.astype(BF16), v)
    jc = lax.broadcasted_iota(jnp.int32, (n, 1), 0).astype(F32)
    o = o + _dot(q, s.astype(BF16)) * jnp.exp((jc + 1.0) * lg)
    ks = (k * jnp.exp((n - 1.0 - jc) * lg)).astype(BF16)
    kv = lax.dot_general(ks, v, (((0,), (0,)), ((), ())), preferred_element_type=F32)
    s_new = jnp.exp(n * lg) * s + kv
    return s_new, o


def _group_norm(o, g):
    mu = jnp.mean(o, axis=-1, keepdims=True)
    oc = o - mu
    var = jnp.mean(oc * oc, axis=-1, keepdims=True)
    return oc * lax.rsqrt(var + EPS) * g


def _head_log_gamma(h):
    hv = jnp.full((1, 1), h, jnp.int32)
    lg = jnp.full((1, 1), _LOG_GAMMA[H_C - 1], F32)
    for i in range(H_C - 2, -1, -1):
        lg = jnp.where(hv == i, jnp.float32(_LOG_GAMMA[i]), lg)
    return lg


def _ret_prompt_kernel(q_ref, k_ref, v_ref, g_ref, o_ref, s_ref, *, seq):
    lg = _head_log_gamma(pl.program_id(1))
    g = g_ref[...]
    s0 = jnp.zeros((DK_C, DV_C), F32)
    s, o = _ret_chunk(s0, q_ref[0:N_META, :], k_ref[0:N_META, :], v_ref[0:N_META, :], N_META, lg)
    o_ref[0:N_META, :] = _group_norm(o, g)

    def chunk(i, s):
        r0 = pl.multiple_of(N_META + i * RET_BLK, N_META)
        s, o = _ret_chunk(s, q_ref[pl.ds(r0, RET_BLK), :], k_ref[pl.ds(r0, RET_BLK), :],
                          v_ref[pl.ds(r0, RET_BLK), :], RET_BLK, lg)
        o_ref[pl.ds(r0, RET_BLK), :] = _group_norm(o, g)
        return s

    s_ref[...] = lax.fori_loop(0, (seq - N_META) // RET_BLK, chunk, s)


def _ret_prompt(qr, kr, vc, l, gn_g):
    b, seq, _ = qr.shape
    assert (seq - N_META) % RET_BLK == 0
    qk = pl.BlockSpec((None, seq, DK_C), lambda i, h: (i, 0, h))
    vv = pl.BlockSpec((None, seq, DV_C), lambda i, h: (i, 0, h))
    return pl.pallas_call(
        functools.partial(_ret_prompt_kernel, seq=seq),
        grid=(b, H_C),
        in_specs=[qk, qk, vv, pl.BlockSpec((None, 1, DV_C), lambda i, h: (l, 0, h))],
        out_specs=[vv, pl.BlockSpec((None, None, DK_C, DV_C), lambda i, h: (i, h, 0, 0))],
        out_shape=[jax.ShapeDtypeStruct((b, seq, D_V_C), F32),
                   jax.ShapeDtypeStruct((b, H_C, DK_C, DV_C), F32)],
        compiler_params=_cparams(2),
        name="ret_prompt",
    )(qr, kr, vc, gn_g)


def _ret_sample_kernel(q_ref, k_ref, v_ref, g_ref, s0_ref, o_ref, s_ref, *, nb, n_tok):
    lg = _head_log_gamma(pl.program_id(1))
    g = g_ref[...]
    for s in range(nb):
        s_new, o = _ret_chunk(s0_ref[s], q_ref[s], k_ref[s], v_ref[s], n_tok, lg)
        o_ref[s] = _group_norm(o, g)
        s_ref[s] = s_new


def _ret_sample(qr, kr, vc, l, gn_g, state, nb):
    n_seq, n_tok, _ = qr.shape
    qk = pl.BlockSpec((nb, n_tok, DK_C), lambda i, h: (i, 0, h))
    vv = pl.BlockSpec((nb, n_tok, DV_C), lambda i, h: (i, 0, h))
    return pl.pallas_call(
        functools.partial(_ret_sample_kernel, nb=nb, n_tok=n_tok),
        grid=(n_seq // nb, H_C),
        in_specs=[qk, qk, vv, pl.BlockSpec((None, 1, DV_C), lambda i, h: (l, 0, h)),
                  pl.BlockSpec((None, nb, None, DK_C, DV_C), lambda i, h: (l, i, h, 0, 0))],
        out_specs=[vv, pl.BlockSpec((nb, None, DK_C, DV_C), lambda i, h: (i, h, 0, 0))],
        out_shape=[jax.ShapeDtypeStruct((n_seq, n_tok, D_V_C), F32),
                   jax.ShapeDtypeStruct((n_seq, H_C, DK_C, DV_C), F32)],
        compiler_params=_cparams(2),
        name="ret_sample",
    )(qr, kr, vc, gn_g, state)


def _merge_kernel(x_ref, oa_ref, cb_ref, on_ref, gpre_ref, gpost_ref, wgc_ref, wgate_ref, wao_ref,
                  wpw_ref, wro_ref, wo_ref, y_ref):
    x = x_ref[...]
    h = _rms(x, gpre_ref[...]).astype(BF16)
    zc = _dot(h, wgc_ref[...])
    y_c = _dot((zc * _sigmoid(zc) * on_ref[...]).astype(BF16), wro_ref[...])
    y_a = _dot(oa_ref[...], wao_ref[...])
    y_b = _dot(cb_ref[...], wpw_ref[...])
    merged = (_sigmoid(_dot(h, wgate_ref[:, 0:D_MODEL])) * y_a
              + _sigmoid(_dot(h, wgate_ref[:, D_MODEL:2 * D_MODEL])) * y_b
              + _sigmoid(_dot(h, wgate_ref[:, 2 * D_MODEL:])) * y_c)
    out = _dot(merged.astype(BF16), wo_ref[...])
    y_ref[...] = x + _rms(out, gpost_ref[...])


def _merge(x, oa, cb, on, l, gpre, gpost, wgc, wgate, wao, wpw, wro, wo, tm):
    t = x.shape[0]
    row = lambda n: pl.BlockSpec((tm, n), lambda i: (i, 0))
    w = lambda k, n: _resident((None, k, n), lambda i: (l, 0, 0))
    return pl.pallas_call(
        _merge_kernel,
        grid=(t // tm,),
        in_specs=[row(D_MODEL), row(D_A), row(C_B), row(D_V_C), w(1, D_MODEL), w(1, D_MODEL),
                  w(D_MODEL, D_V_C), w(D_MODEL, N_BRANCH * D_MODEL), w(D_A, D_MODEL), w(C_B, D_MODEL),
                  w(D_V_C, D_MODEL), w(D_MODEL, D_MODEL)],
        out_specs=row(D_MODEL),
        out_shape=jax.ShapeDtypeStruct((t, D_MODEL), F32),
        compiler_params=_cparams(1),
        name="merge",
    )(x, oa, cb, on, gpre, gpost, wgc, wgate, wao, wpw, wro, wo)


FF_CHUNK = 1024


def _ffn_kernel(x_ref, gpre_ref, gpost_ref, wup_ref, wdn_ref, y_ref):
    x = x_ref[...]
    h = _rms(x, gpre_ref[...]).astype(BF16)
    ff = jnp.zeros(x.shape, F32)
    for c in range(D_FF // FF_CHUNK):
        up = jnp.maximum(_dot(h, wup_ref[:, c * FF_CHUNK:(c + 1) * FF_CHUNK]), 0.0)
        ff = ff + _dot((up * up).astype(BF16), wdn_ref[c * FF_CHUNK:(c + 1) * FF_CHUNK, :])
    y_ref[...] = x + _rms(ff, gpost_ref[...])


def _ffn(x, l, gpre, gpost, wup, wdn, tm):
    t = x.shape[0]
    row = pl.BlockSpec((tm, D_MODEL), lambda i: (i, 0))
    w = lambda k, n: _resident((None, k, n), lambda i: (l, 0, 0))
    return pl.pallas_call(
        _ffn_kernel,
        grid=(t // tm,),
        in_specs=[row, w(1, D_MODEL), w(1, D_MODEL), w(D_MODEL, D_FF), w(D_FF, D_MODEL)],
        out_specs=row,
        out_shape=jax.ShapeDtypeStruct((t, D_MODEL), F32),
        compiler_params=_cparams(1),
        name="ffn",
    )(x, gpre, gpost, wup, wdn)


def _rope_tables(pos):
    half = DK_C // 2
    inv = ROPE_BASE ** (-jnp.arange(half, dtype=F32) / half)
    ang = pos.astype(F32)[:, None] * inv[None, :]
    cos, sin = jnp.cos(ang), jnp.sin(ang)
    return jnp.concatenate([cos, cos], axis=1), jnp.concatenate([-sin, sin], axis=1)


def _head_scan_matrices():
    b = np.arange(LANES)
    same = (b[:, None] % H_A) == (b[None, :] % H_A)
    t_inc = same & (b[:, None] <= b[None, :])
    return jnp.asarray(np.concatenate([t_inc, same], axis=1).astype(np.float32), dtype=BF16)


def kernel(x_prompt, x_sample, cache_k, cache_v, cache_logf, state_conv, state_ret, page_table,
           meta_tokens, norm_mix_pre, norm_mix_post, norm_ffn_pre, norm_ffn_post,
           w_in, b_forget, w_dw, b_dw, ln_conv_g, ln_conv_b, w_pw_out, w_attn_out,
           gn_ret_g, w_ret_out, w_o, w_ff_up, w_ff_down):
    depth = w_in.shape[0]
    bp, seq_p, _ = x_prompt.shape
    n_dec, n_tok, _ = x_sample.shape
    n_pages = page_table.shape[1]
    n_pool, page = cache_k.shape[1], cache_k.shape[2]
    past_len = n_pages * page
    lp = N_META + seq_p
    tp, ts = bp * lp, n_dec * n_tok

    cuts = np.cumsum([0, D_A, D_A, D_A, H_A, 2 * C_B, D_QK_C, D_QK_C, D_V_C, D_V_C, N_BRANCH * D_MODEL])
    seg = lambda a, b: w_in[:, :, int(cuts[a]):int(cuts[b])].astype(BF16)
    wa, wglu, wqk, wv, wgc, wgate = seg(0, 3), seg(4, 5), seg(5, 7), seg(7, 8), seg(8, 9), seg(9, 10)
    wf = jnp.pad(w_in[:, :, int(cuts[3]):int(cuts[4])], ((0, 0), (0, 0), (0, LANES - H_A))).astype(BF16)
    bfp = jnp.pad(b_forget, ((0, 0), (0, LANES - H_A)))[:, None, :]
    wao, wpw, wro, wo = (w.astype(BF16) for w in (w_attn_out, w_pw_out, w_ret_out, w_o))
    wup, wdn = w_ff_up.astype(BF16), w_ff_down.astype(BF16)
    vec = lambda a: a[:, None, :]
    g_mpre, g_mpost, g_fpre, g_fpost = vec(norm_mix_pre), vec(norm_mix_post), vec(norm_ffn_pre), vec(norm_ffn_post)
    wdw = jnp.pad(w_dw, ((0, 0), (0, CONV_PAD - CONV_W), (0, 0)))
    bdw, lng, lnb, gng = vec(b_dw), vec(ln_conv_g), vec(ln_conv_b), vec(gn_ret_g)

    tm_in_p = _row_tile(tp, 400, must_divide=lp)
    tm_in_s = _row_tile(ts, 256)
    tm_p = _row_tile(tp, 400)
    tm_s = _row_tile(ts, 512)
    cos_p, sin_p = _rope_tables(jnp.arange(lp))
    cos_s, sin_s = _rope_tables(past_len + jnp.arange(n_tok))
    cos_s, sin_s = (jnp.tile(a, (tm_in_s // n_tok, 1)) for a in (cos_s, sin_s))

    tmat = _head_scan_matrices()
    rows = page * H_A // LANES
    cin, rtot = _pool_cumsum(cache_logf.reshape(depth * n_pool * rows, LANES), tmat)
    cin = cin.reshape(depth, n_pool, rows, LANES)
    rtot = rtot.reshape(depth, n_pool, rows, LANES)
    page_flat = page_table.reshape(-1)
    t_inc = tmat[:, :LANES]

    xp = jnp.concatenate([jnp.broadcast_to(meta_tokens[None], (bp, N_META, D_MODEL)), x_prompt], axis=1)
    xp = xp.reshape(tp, D_MODEL)
    xs = x_sample.reshape(ts, D_MODEL)
    buf0 = jnp.zeros((bp, CONV_W - 1, C_B), F32)
    nq = H_A * n_tok

    outs = [[] for _ in range(10)]
    for l in range(depth):
        qa, ka, va, lf, u, qr, kr, vc = _inproj(xp, l, g_mpre, wa, wf, bfp, wglu, wqk, wv, cos_p, sin_p, tm_in_p)
        r3 = lambda a: a.reshape(bp, lp, a.shape[-1])
        oa = _fox_prompt(r3(qa), r3(ka), r3(va), r3(lf))
        cact, nbuf = _conv(r3(u), buf0, l, wdw, bdw, lng, lnb, 1)
        on, s_p = _ret_prompt(r3(qr), r3(kr), r3(vc), l, gng)
        xp = _merge(xp, oa.reshape(tp, D_A), cact.reshape(tp, C_B), on.reshape(tp, D_V_C), l,
                    g_mpre, g_mpost, wgc, wgate, wao, wpw, wro, wo, tm_p)
        xp = _ffn(xp, l, g_fpre, g_fpost, wup, wdn, tm_p)
        outs[0].append(ka.reshape(bp, lp, H_A, DH_A))
        outs[1].append(va.reshape(bp, lp, H_A, DH_A))
        outs[2].append(lf[:, :H_A].reshape(bp, lp, H_A))
        outs[3].append(nbuf)
        outs[4].append(s_p)

        qa, ka, va, lf, u, qr, kr, vc = _inproj(xs, l, g_mpre, wa, wf, bfp, wglu, wqk, wv, cos_s, sin_s, tm_in_s)
        s3 = lambda a: a.reshape(n_dec, n_tok, a.shape[-1])
        q_hq = qa.reshape(n_dec, n_tok, H_A, DH_A).transpose(0, 2, 1, 3).reshape(n_dec, nq, DH_A)
        lf8 = lf[:, :H_A]
        lf_new = jnp.pad(lf8.reshape(n_dec, 1, nq), ((0, 0), (0, 0), (0, LANES - nq)))
        o_hq = _fox_sample(l, page_flat, q_hq, cache_k, cache_v, cin, rtot,
                           ka.reshape(n_dec, nq, DH_A), va.reshape(n_dec, nq, DH_A), lf_new, t_inc, n_pages)
        oa = o_hq.reshape(n_dec, H_A, n_tok, DH_A).transpose(0, 2, 1, 3).reshape(ts, D_A)
        cact, nbuf = _conv(s3(u), state_conv, l, wdw, bdw, lng, lnb, 16)
        on, s_s = _ret_sample(s3(qr), s3(kr), s3(vc), l, gng, state_ret, 8)
        xs = _merge(xs, oa, cact.reshape(ts, C_B), on.reshape(ts, D_V_C), l,
                    g_mpre, g_mpost, wgc, wgate, wao, wpw, wro, wo, tm_s)
        xs = _ffn(xs, l, g_fpre, g_fpost, wup, wdn, tm_s)
        outs[5].append(ka.reshape(n_dec, n_tok, H_A, DH_A))
        outs[6].append(va.reshape(n_dec, n_tok, H_A, DH_A))
        outs[7].append(lf8.reshape(n_dec, n_tok, H_A))
        outs[8].append(nbuf)
        outs[9].append(s_s)

    y_prompt = xp.reshape(bp, lp, D_MODEL)[:, N_META:]
    y_sample = xs.reshape(n_dec, n_tok, D_MODEL)
    return (y_prompt, y_sample) + tuple(jnp.stack(o) for o in outs)
```

```python
import functools

import numpy as np
import jax
import jax.numpy as jnp
from jax import lax
from jax.experimental import pallas as pl
from jax.experimental.pallas import tpu as pltpu

D_MODEL = 1024
N_META = 16
H_A = 8
DH_A = 64
D_A = H_A * DH_A
C_B = 512
CONV_W = 31
H_C = 4
DK_C = 128
DV_C = 256
D_QK_C = H_C * DK_C
D_V_C = H_C * DV_C
N_BRANCH = 3
D_FF = 4 * D_MODEL
EPS = 1e-6
ROPE_BASE = 10000.0

LANES = 128
SUBLANES = 8
ATT_BLK = 256
RET_BLK = 128
NEG = -1e30
VMEM_LIMIT = 56 * 1024 * 1024
PAGES_PER_STEP = 8

BF16 = jnp.bfloat16
F32 = jnp.float32

_LOG_GAMMA = tuple(float(np.log1p(-(2.0 ** (-5.0 - h)))) for h in range(H_C))


def _cparams(n_axes):
    return pltpu.CompilerParams(dimension_semantics=("arbitrary",) * n_axes,
                                vmem_limit_bytes=VMEM_LIMIT)


def _row_tile(total, target, must_divide=None):
    best = None
    for t in range(SUBLANES, min(total, target) + 1, SUBLANES):
        if total % t == 0 and (must_divide is None or must_divide % t == 0):
            best = t
    assert best is not None, (total, target, must_divide)
    return best


def _resident(shape, index_map):
    return pl.BlockSpec(shape, index_map, pipeline_mode=pl.Buffered(1))


def _dot(a, b):
    return jnp.dot(a, b, preferred_element_type=F32)


def _dot_nt(a, b):
    return lax.dot_general(a, b, (((1,), (1,)), ((), ())), preferred_element_type=F32)


def _split3(x):
    hi = x.astype(BF16)
    r1 = x - hi.astype(F32)
    mid = r1.astype(BF16)
    r2 = r1 - mid.astype(F32)
    return hi, mid, r2.astype(BF16)


def _rms(x, g):
    return x * lax.rsqrt(jnp.mean(x * x, axis=-1, keepdims=True) + EPS) * g


def _log_sigmoid(z):
    return jnp.minimum(z, 0.0) - jnp.log1p(jnp.exp(-jnp.abs(z)))


def _sigmoid(z):
    return 1.0 / (1.0 + jnp.exp(-z))


def _inproj_kernel(x_ref, g_ref, wa_ref, wf_ref, bf_ref, wglu_ref, wqk_ref, wv_ref, cos_ref, sin_ref,
                   qa_ref, ka_ref, va_ref, lf_ref, u_ref, qr_ref, kr_ref, vc_ref):
    h = _rms(x_ref[...], g_ref[...]).astype(BF16)
    za = _dot(h, wa_ref[...])
    qa_ref[...] = (za[:, :D_A] * (DH_A ** -0.5)).astype(BF16)
    ka_ref[...] = za[:, D_A:2 * D_A]
    va_ref[...] = za[:, 2 * D_A:]
    lf_ref[...] = _log_sigmoid(_dot(h, wf_ref[...]) + bf_ref[...])
    zg = _dot(h, wglu_ref[...])
    u_ref[...] = zg[:, :C_B] * _sigmoid(zg[:, C_B:])
    zqk = _dot(h, wqk_ref[...])
    cos = cos_ref[...]
    sin = sin_ref[...]
    for j in range(2 * H_C):
        xh = zqk[:, j * DK_C:(j + 1) * DK_C]
        r = xh * cos + pltpu.roll(xh, DK_C // 2, 1) * sin
        if j < H_C:
            qr_ref[:, j * DK_C:(j + 1) * DK_C] = r.astype(BF16)
        else:
            kr_ref[:, (j - H_C) * DK_C:(j - H_C + 1) * DK_C] = r * (DK_C ** -0.5)
    vc_ref[...] = _dot(h, wv_ref[...]).astype(BF16)


def _inproj(x, l, gpre, wa, wf, bfp, wglu, wqk, wv, cos_t, sin_t, tm):
    t = x.shape[0]
    n_tab = cos_t.shape[0] // tm
    row = lambda n: pl.BlockSpec((tm, n), lambda i: (i, 0))
    wspec = lambda n: _resident((None, D_MODEL, n), lambda i: (l, 0, 0))
    tab = pl.BlockSpec((tm, LANES), lambda i: (i % n_tab, 0))
    outs = [(D_A, BF16), (D_A, F32), (D_A, F32), (LANES, F32), (C_B, F32),
            (D_QK_C, BF16), (D_QK_C, F32), (D_V_C, BF16)]
    return pl.pallas_call(
        _inproj_kernel,
        grid=(t // tm,),
        in_specs=[row(D_MODEL), _resident((None, 1, D_MODEL), lambda i: (l, 0, 0)),
                  wspec(3 * D_A), wspec(LANES), _resident((None, 1, LANES), lambda i: (l, 0, 0)),
                  wspec(2 * C_B), wspec(2 * D_QK_C), wspec(D_V_C), tab, tab],
        out_specs=[row(n) for n, _ in outs],
        out_shape=[jax.ShapeDtypeStruct((t, n), d) for n, d in outs],
        compiler_params=_cparams(1),
        name="inproj",
    )(x, gpre, wa, wf, bfp, wglu, wqk, wv, cos_t, sin_t)


N_AUG = 3
PACK_ONE = N_AUG * H_A


def _fox_aug_select():
    sel = np.zeros((H_A // 2, LANES, 4 * LANES), np.float32)
    for p in range(H_A // 2):
        for hh in range(2):
            h, x0 = 2 * p + hh, (1 - hh) * DH_A
            qb, kb = hh * LANES + x0, 2 * LANES + hh * LANES + x0
            for j in range(N_AUG):
                sel[p, j * H_A + h, qb + j] = 1.0
                sel[p, PACK_ONE, qb + N_AUG + j] = 1.0
                sel[p, PACK_ONE, kb + j] = 1.0
                sel[p, j * H_A + h, kb + N_AUG + j] = -1.0
    return jnp.asarray(sel, dtype=BF16)


def _fox_prompt_kernel(q_ref, k_ref, v_ref, lf_ref, sel_ref, o_ref, c_scr, qaug, kaug, vaug, *, seq):
    n_full = seq // LANES
    tail = seq - n_full * LANES
    n_blk = (seq - N_META) // ATT_BLK

    r = lax.broadcasted_iota(jnp.int32, (LANES, LANES), 0)
    c = lax.broadcasted_iota(jnp.int32, (LANES, LANES), 1)
    tri = jnp.where(c <= r, 1.0, 0.0).astype(BF16)
    carry = jnp.zeros((1, LANES), F32)
    for b in range(n_full + (1 if tail else 0)):
        nb = LANES if b < n_full else tail
        hi, mid, lo = _split3(lf_ref[b * LANES:b * LANES + nb, :])
        tb = tri[:nb, :nb]
        cb = _dot(tb, hi) + _dot(tb, mid) + _dot(tb, lo) + carry
        c_scr[b * LANES:b * LANES + nb, :] = cb
        carry = cb[nb - 1:nb, :]

    lane = lax.broadcasted_iota(jnp.int32, (seq, LANES), 1)
    cc = c_scr[...]
    hi = cc.astype(BF16).astype(F32)
    r1 = cc - hi
    mid = r1.astype(BF16).astype(F32)
    lo = r1 - mid
    packed = jnp.where(lane < H_A, hi,
                       jnp.where(lane < 2 * H_A, pltpu.roll(mid, H_A, 1),
                                 jnp.where(lane < PACK_ONE, pltpu.roll(lo, 2 * H_A, 1),
                                           jnp.where(lane == PACK_ONE, 1.0, 0.0)))).astype(BF16)
    for p in range(H_A // 2):
        aug = _dot(packed, sel_ref[p])
        qp = q_ref[:, p * LANES:(p + 1) * LANES].astype(F32)
        kp = k_ref[:, p * LANES:(p + 1) * LANES]
        vp = v_ref[:, p * LANES:(p + 1) * LANES]
        for hh in range(2):
            h, x0 = 2 * p + hh, (1 - hh) * DH_A
            own = (lane >= hh * DH_A) & (lane < (hh + 1) * DH_A)
            qaug[h] = jnp.where(own, qp, aug[:, hh * LANES:(hh + 1) * LANES]).astype(BF16)
            kaug[h] = jnp.where(own, kp, aug[:, (2 + hh) * LANES:(3 + hh) * LANES]).astype(BF16)
            vaug[h] = jnp.where(own, vp, jnp.where(lane == x0, 1.0, 0.0)).astype(BF16)

    def first(h, qa, nq):
        s = _dot_nt(qa, kaug[h, 0:N_META, :])
        if nq == N_META:
            rr = lax.broadcasted_iota(jnp.int32, s.shape, 0)
            cc2 = lax.broadcasted_iota(jnp.int32, s.shape, 1)
            s = jnp.where(rr >= cc2, s, NEG)
        m = jnp.max(s, axis=1, keepdims=True)
        acc = _dot(jnp.exp(s - m).astype(BF16), vaug[h, 0:N_META, :])
        return m, acc

    def steps(q0, start, carry, causal):
        qs = [qaug[h, pl.ds(q0, ATT_BLK), :] for h in range(H_A)]
        ks = [kaug[h, pl.ds(start, ATT_BLK), :] for h in range(H_A)]
        vs = [vaug[h, pl.ds(start, ATT_BLK), :] for h in range(H_A)]
        new = []
        for h in range(H_A):
            m_old, acc = carry[h]
            s = _dot_nt(qs[h], ks[h])
            if causal:
                rr = lax.broadcasted_iota(jnp.int32, s.shape, 0)
                cc2 = lax.broadcasted_iota(jnp.int32, s.shape, 1)
                s = jnp.where(rr >= cc2, s, NEG)
            m_new = jnp.maximum(m_old, jnp.max(s, axis=1, keepdims=True))
            pr = jnp.exp(s - m_new).astype(BF16)
            new.append((m_new, jnp.exp(m_old - m_new) * acc + _dot(pr, vs[h])))
        return tuple(new)

    def normalised(acc0, acc1, nq):
        ln = lax.broadcasted_iota(jnp.int32, (nq, LANES), 1)
        return jnp.where(ln < DH_A, acc0 / acc0[:, DH_A:DH_A + 1], acc1 / acc1[:, 0:1])

    for p in range(H_A // 2):
        a0 = first(2 * p, qaug[2 * p, 0:N_META, :], N_META)[1]
        a1 = first(2 * p + 1, qaug[2 * p + 1, 0:N_META, :], N_META)[1]
        o_ref[0:N_META, p * LANES:(p + 1) * LANES] = normalised(a0, a1, N_META).astype(o_ref.dtype)

    def q_block(i, _):
        q0 = pl.multiple_of(N_META + i * ATT_BLK, N_META)
        carry = tuple(first(h, qaug[h, pl.ds(q0, ATT_BLK), :], ATT_BLK) for h in range(H_A))

        def kv_block(j, carry):
            return steps(q0, pl.multiple_of(N_META + j * ATT_BLK, N_META), carry, False)

        carry = lax.fori_loop(0, i, kv_block, carry)
        carry = steps(q0, q0, carry, True)
        for p in range(H_A // 2):
            o_ref[pl.ds(q0, ATT_BLK), p * LANES:(p + 1) * LANES] = normalised(
                carry[2 * p][1], carry[2 * p + 1][1], ATT_BLK).astype(o_ref.dtype)
        return 0

    lax.fori_loop(0, n_blk, q_block, 0)


def _fox_prompt(qa, ka, va, lf, sel):
    b, seq, _ = qa.shape
    assert (seq - N_META) % ATT_BLK == 0
    full = lambda n: pl.BlockSpec((None, seq, n), lambda i: (i, 0, 0))
    heads = lambda: pltpu.VMEM((H_A, seq, LANES), BF16)
    return pl.pallas_call(
        functools.partial(_fox_prompt_kernel, seq=seq),
        grid=(b,),
        in_specs=[full(D_A), full(D_A), full(D_A), full(LANES),
                  _resident(sel.shape, lambda i: (0, 0, 0))],
        out_specs=full(D_A),
        out_shape=jax.ShapeDtypeStruct((b, seq, D_A), BF16),
        scratch_shapes=[pltpu.VMEM((seq, LANES), F32), heads(), heads(), heads()],
        compiler_params=_cparams(1),
        name="fox_prompt",
    )(qa, ka, va, lf, sel)


def _lane_cumsum_kernel(lf_ref, t_ref, c_ref):
    hi, mid, lo = _split3(lf_ref[...])
    t = t_ref[...]
    c_ref[...] = _dot(hi, t) + _dot(mid, t) + _dot(lo, t)


def _lane_cumsum(x, tri_u):
    rows = x.shape[0]
    tm = _row_tile(rows, 4096)
    spec = pl.BlockSpec((tm, LANES), lambda i: (i, 0))
    return pl.pallas_call(
        _lane_cumsum_kernel,
        grid=(rows // tm,),
        in_specs=[spec, _resident((LANES, LANES), lambda i: (0, 0))],
        out_specs=spec,
        out_shape=jax.ShapeDtypeStruct((rows, LANES), F32),
        compiler_params=_cparams(1),
        name="lane_cumsum",
    )(x, tri_u)


def _fox_sample_kernel(pt_ref, q_ref, *refs, n_steps, g_pages, n_tok):
    del pt_ref
    k_refs, v_refs, c_refs = refs[:g_pages], refs[g_pages:2 * g_pages], refs[2 * g_pages:3 * g_pages]
    kn_ref, vn_ref, lfn_ref, t_ref, o_ref, m_scr, l_scr, acc_scr, carry_scr = refs[3 * g_pages:]
    p = pl.program_id(1)
    page = k_refs[0].shape[-1]
    nq = q_ref.shape[0]
    q = q_ref[...]

    def update(s, pv):
        m_old = m_scr[...]
        m_new = jnp.maximum(m_old, jnp.max(s, axis=1, keepdims=True))
        alpha = jnp.exp(m_old - m_new)
        pr = jnp.exp(s - m_new)
        m_scr[...] = m_new
        l_scr[...] = alpha * l_scr[...] + jnp.sum(pr, axis=1, keepdims=True)
        acc_scr[...] = alpha * acc_scr[...] + pv(pr.astype(BF16))

    @pl.when(p == 0)
    def _():
        m_scr[...] = jnp.full(m_scr.shape, NEG, F32)
        l_scr[...] = jnp.zeros(l_scr.shape, F32)
        acc_scr[...] = jnp.zeros(acc_scr.shape, F32)
        carry_scr[...] = jnp.zeros(carry_scr.shape, F32)
        hi, mid, lo = _split3(jnp.broadcast_to(lfn_ref[...], (SUBLANES, LANES)))
        t = t_ref[...]
        c_new = (_dot(hi, t) + _dot(mid, t) + _dot(lo, t))[0:1, :nq]
        sn = _dot_nt(q, kn_ref[...]) - c_new
        rr = lax.broadcasted_iota(jnp.int32, sn.shape, 0)
        cc = lax.broadcasted_iota(jnp.int32, sn.shape, 1)
        ok = (rr // n_tok == cc // n_tok) & (cc % n_tok <= rr % n_tok)
        update(jnp.where(ok, sn, NEG), lambda pb: _dot(pb, vn_ref[...]))

    def per_head_rows(x):
        return jnp.concatenate([jnp.broadcast_to(x[h:h + 1, :], (n_tok, x.shape[1])) for h in range(H_A)], axis=0)

    carry = carry_scr[...]
    bias, ks, vs = [], [], []
    for g in range(g_pages):
        cin = c_refs[g][...]
        tot = cin[:, page - 1:page]
        bias.append(per_head_rows((tot - cin) + carry))
        carry = carry + tot
        ks.append(k_refs[g][...].reshape(D_A, page).astype(BF16))
        vs.append(v_refs[g][...].reshape(D_A, page).astype(BF16))
    carry_scr[...] = carry
    v_all = jnp.concatenate(vs, axis=1)
    s = _dot(q, jnp.concatenate(ks, axis=1)) + jnp.concatenate(bias, axis=1)
    update(s, lambda pb: _dot_nt(pb, v_all))

    @pl.when(p == n_steps - 1)
    def _():
        acc = acc_scr[...] / l_scr[...]
        o_ref[...] = jnp.concatenate(
            [acc[h * n_tok:(h + 1) * n_tok, h * DH_A:(h + 1) * DH_A] for h in range(H_A)], axis=0).astype(o_ref.dtype)


def _fox_sample(l, page_flat, q_hq, cache_kt, cache_vt, cpool, k_new, v_new, lf_new, tri_u, n_pages):
    n_seq, nq, _ = q_hq.shape
    page = cache_kt.shape[-1]
    g_pages = max(g for g in range(1, PAGES_PER_STEP + 1) if n_pages % g == 0)
    n_steps = n_pages // g_pages
    slot = lambda s, p, g: s * n_pages + (n_pages - 1 - (p * g_pages + g))
    idx = lambda g: (lambda s, p, pt: (l, pt[slot(s, p, g)], 0, 0, 0))
    cidx = lambda g: (lambda s, p, pt: (l, pt[slot(s, p, g)], 0, 0))
    kv_specs = [pl.BlockSpec((None, None, H_A, DH_A, page), idx(g)) for g in range(g_pages)]
    c_specs = [pl.BlockSpec((None, None, H_A, page), cidx(g)) for g in range(g_pages)]
    seq_spec = lambda a, b: pl.BlockSpec((None, a, b), lambda s, p, pt: (s, 0, 0))
    grid_spec = pltpu.PrefetchScalarGridSpec(
        num_scalar_prefetch=1,
        grid=(n_seq, n_steps),
        in_specs=[seq_spec(nq, D_A)] + kv_specs + kv_specs + c_specs
                 + [seq_spec(nq, D_A), seq_spec(nq, D_A), seq_spec(1, LANES),
                    pl.BlockSpec((LANES, LANES), lambda s, p, pt: (0, 0))],
        out_specs=seq_spec(nq, DH_A),
        scratch_shapes=[pltpu.VMEM((nq, 1), F32), pltpu.VMEM((nq, 1), F32),
                        pltpu.VMEM((nq, D_A), F32), pltpu.VMEM((H_A, 1), F32)],
    )
    return pl.pallas_call(
        functools.partial(_fox_sample_kernel, n_steps=n_steps, g_pages=g_pages, n_tok=nq // H_A),
        grid_spec=grid_spec,
        out_shape=jax.ShapeDtypeStruct((n_seq, nq, DH_A), BF16),
        compiler_params=_cparams(2),
        name="fox_sample",
    )(page_flat, q_hq, *([cache_kt] * g_pages), *([cache_vt] * g_pages), *([cpool] * g_pages),
      k_new, v_new, lf_new, tri_u)


CONV_PAD = 32


def _conv_kernel(u_ref, buf_ref, w_ref, b_ref, g_ref, beta_ref, act_ref, nbuf_ref, ext, dw, *, nb, seq, tr):
    n_tiles = -(-seq // tr)
    keep = CONV_W - 1
    for s in range(nb):
        ext[s, 0:CONV_PAD - keep, :] = jnp.zeros((CONV_PAD - keep, C_B), F32)
        ext[s, CONV_PAD - keep:CONV_PAD, :] = buf_ref[s]
        ext[s, CONV_PAD:CONV_PAD + seq, :] = u_ref[s]
        if n_tiles * tr > seq:
            ext[s, CONV_PAD + seq:CONV_PAD + n_tiles * tr, :] = jnp.zeros((n_tiles * tr - seq, C_B), F32)
        nbuf_ref[s] = ext[s, CONV_PAD + seq - keep:CONV_PAD + seq, :]

        def tile(i, _, s=s):
            r0 = pl.multiple_of(i * tr, SUBLANES)
            for c in range(C_B // LANES):
                win = ext[s, pl.ds(r0, tr + CONV_PAD), c * LANES:(c + 1) * LANES]
                acc = jnp.broadcast_to(b_ref[:, c * LANES:(c + 1) * LANES], (tr, LANES))
                for j in range(CONV_W):
                    off = CONV_PAD - keep + j
                    acc = acc + win[off:off + tr, :] * w_ref[j:j + 1, c * LANES:(c + 1) * LANES]
                dw[pl.ds(r0, tr), c * LANES:(c + 1) * LANES] = acc
            return 0

        lax.fori_loop(0, n_tiles, tile, 0)

        def norm(r0, n, s=s):
            x = dw[pl.ds(r0, n), :]
            mu = jnp.mean(x, axis=-1, keepdims=True)
            xc = x - mu
            var = jnp.mean(xc * xc, axis=-1, keepdims=True)
            y = xc * lax.rsqrt(var + EPS) * g_ref[...] + beta_ref[...]
            act_ref[s, pl.ds(r0, n), :] = (y * _sigmoid(y)).astype(act_ref.dtype)

        n_full = seq // tr

        def norm_tile(i, _):
            norm(pl.multiple_of(i * tr, SUBLANES), tr)
            return 0

        lax.fori_loop(0, n_full, norm_tile, 0)
        if seq > n_full * tr:
            norm(n_full * tr, seq - n_full * tr)


def _conv(u, buf, l, w_dw, b_dw, ln_g, ln_b, nb):
    bsz, seq, _ = u.shape
    tr = LANES if seq >= LANES else seq
    n_tiles = -(-seq // tr)
    keep = CONV_W - 1
    par = lambda r: pl.BlockSpec((None, r, C_B), lambda i: (l, 0, 0))
    if buf.ndim == 4:
        buf_spec = pl.BlockSpec((None, nb, keep, C_B), lambda i: (l, i, 0, 0))
    else:
        buf_spec = pl.BlockSpec((nb, keep, C_B), lambda i: (i, 0, 0))
    return pl.pallas_call(
        functools.partial(_conv_kernel, nb=nb, seq=seq, tr=tr),
        grid=(bsz // nb,),
        in_specs=[pl.BlockSpec((nb, seq, C_B), lambda i: (i, 0, 0)), buf_spec,
                  par(CONV_PAD), par(1), par(1), par(1)],
        out_specs=[pl.BlockSpec((nb, seq, C_B), lambda i: (i, 0, 0)),
                   pl.BlockSpec((nb, keep, C_B), lambda i: (i, 0, 0))],
        out_shape=[jax.ShapeDtypeStruct((bsz, seq, C_B), BF16),
                   jax.ShapeDtypeStruct((bsz, keep, C_B), F32)],
        scratch_shapes=[pltpu.VMEM((nb, CONV_PAD + n_tiles * tr, C_B), F32),
                        pltpu.VMEM((n_tiles * tr, C_B), F32)],
        compiler_params=_cparams(1),
        name="conv",
    )(u, buf, w_dw, b_dw, ln_g, ln_b)


def _ret_chunk(s, q, k, v, n, lg):
    ii = lax.broadcasted_iota(jnp.int32, (n, n), 0)
    jj = lax.broadcasted_iota(jnp.int32, (n, n), 1)
    rel = (ii - jj).astype(F32)
    decay = jnp.where(rel >= 0, jnp.exp(jnp.maximum(rel, 0.0) * lg), 0.0)
    scores = _dot_nt(q, k.astype(BF16)) * decay
    o = _dot(scores.astype(BF16), v)
    jc = lax.broadcasted_iota(jnp.int32, (n, 1), 0).astype(F32)
    o = o + _dot(q, s.astype(BF16)) * jnp.exp((jc + 1.0) * lg)
    ks = (k * jnp.exp((n - 1.0 - jc) * lg)).astype(BF16)
    kv = lax.dot_general(ks, v, (((0,), (0,)), ((), ())), preferred_element_type=F32)
    s_new = jnp.exp(n * lg) * s + kv
    return s_new, o


def _group_norm(o, g):
    mu = jnp.mean(o, axis=-1, keepdims=True)
    oc = o - mu
    var = jnp.mean(oc * oc, axis=-1, keepdims=True)
    return oc * lax.rsqrt(var + EPS) * g


def _head_log_gamma(h):
    hv = jnp.full((1, 1), h, jnp.int32)
    lg = jnp.full((1, 1), _LOG_GAMMA[H_C - 1], F32)
    for i in range(H_C - 2, -1, -1):
        lg = jnp.where(hv == i, jnp.float32(_LOG_GAMMA[i]), lg)
    return lg


def _ret_prompt_kernel(q_ref, k_ref, v_ref, g_ref, o_ref, s_ref, *, seq):
    lg = _head_log_gamma(pl.program_id(1))
    g = g_ref[...]
    s0 = jnp.zeros((DK_C, DV_C), F32)
    s, o = _ret_chunk(s0, q_ref[0:N_META, :], k_ref[0:N_META, :], v_ref[0:N_META, :], N_META, lg)
    o_ref[0:N_META, :] = _group_norm(o, g)

    def chunk(i, s):
        r0 = pl.multiple_of(N_META + i * RET_BLK, N_META)
        s, o = _ret_chunk(s, q_ref[pl.ds(r0, RET_BLK), :], k_ref[pl.ds(r0, RET_BLK), :],
                          v_ref[pl.ds(r0, RET_BLK), :], RET_BLK, lg)
        o_ref[pl.ds(r0, RET_BLK), :] = _group_norm(o, g)
        return s

    s_ref[...] = lax.fori_loop(0, (seq - N_META) // RET_BLK, chunk, s)


def _ret_prompt(qr, kr, vc, l, gn_g):
    b, seq, _ = qr.shape
    assert (seq - N_META) % RET_BLK == 0
    qk = pl.BlockSpec((None, seq, DK_C), lambda i, h: (i, 0, h))
    vv = pl.BlockSpec((None, seq, DV_C), lambda i, h: (i, 0, h))
    return pl.pallas_call(
        functools.partial(_ret_prompt_kernel, seq=seq),
        grid=(b, H_C),
        in_specs=[qk, qk, vv, pl.BlockSpec((None, 1, DV_C), lambda i, h: (l, 0, h))],
        out_specs=[vv, pl.BlockSpec((None, None, DK_C, DV_C), lambda i, h: (i, h, 0, 0))],
        out_shape=[jax.ShapeDtypeStruct((b, seq, D_V_C), F32),
                   jax.ShapeDtypeStruct((b, H_C, DK_C, DV_C), F32)],
        compiler_params=_cparams(2),
        name="ret_prompt",
    )(qr, kr, vc, gn_g)


def _ret_sample_kernel(q_ref, k_ref, v_ref, g_ref, s0_ref, o_ref, s_ref, *, nb, n_tok):
    lg = _head_log_gamma(pl.program_id(1))
    g = g_ref[...]
    for s in range(nb):
        s_new, o = _ret_chunk(s0_ref[s], q_ref[s], k_ref[s], v_ref[s], n_tok, lg)
        o_ref[s] = _group_norm(o, g)
        s_ref[s] = s_new


def _ret_sample(qr, kr, vc, l, gn_g, state, nb):
    n_seq, n_tok, _ = qr.shape
    qk = pl.BlockSpec((nb, n_tok, DK_C), lambda i, h: (i, 0, h))
    vv = pl.BlockSpec((nb, n_tok, DV_C), lambda i, h: (i, 0, h))
    return pl.pallas_call(
        functools.partial(_ret_sample_kernel, nb=nb, n_tok=n_tok),
        grid=(n_seq // nb, H_C),
        in_specs=[qk, qk, vv, pl.BlockSpec((None, 1, DV_C), lambda i, h: (l, 0, h)),
                  pl.BlockSpec((None, nb, None, DK_C, DV_C), lambda i, h: (l, i, h, 0, 0))],
        out_specs=[vv, pl.BlockSpec((nb, None, DK_C, DV_C), lambda i, h: (i, h, 0, 0))],
        out_shape=[jax.ShapeDtypeStruct((n_seq, n_tok, D_V_C), F32),
                   jax.ShapeDtypeStruct((n_seq, H_C, DK_C, DV_C), F32)],
        compiler_params=_cparams(2),
        name="ret_sample",
    )(qr, kr, vc, gn_g, state)


def _merge_kernel(x_ref, oa_ref, cb_ref, on_ref, gpre_ref, gpost_ref, wgc_ref, wgate_ref, wao_ref,
                  wpw_ref, wro_ref, wo_ref, y_ref):
    x = x_ref[...]
    h = _rms(x, gpre_ref[...]).astype(BF16)
    zc = _dot(h, wgc_ref[...])
    y_c = _dot((zc * _sigmoid(zc) * on_ref[...]).astype(BF16), wro_ref[...])
    y_a = _dot(oa_ref[...], wao_ref[...])
    y_b = _dot(cb_ref[...], wpw_ref[...])
    merged = (_sigmoid(_dot(h, wgate_ref[:, 0:D_MODEL])) * y_a
              + _sigmoid(_dot(h, wgate_ref[:, D_MODEL:2 * D_MODEL])) * y_b
              + _sigmoid(_dot(h, wgate_ref[:, 2 * D_MODEL:])) * y_c)
    out = _dot(merged.astype(BF16), wo_ref[...])
    y_ref[...] = x + _rms(out, gpost_ref[...])


def _merge(x, oa, cb, on, l, gpre, gpost, wgc, wgate, wao, wpw, wro, wo, tm):
    t = x.shape[0]
    row = lambda n: pl.BlockSpec((tm, n), lambda i: (i, 0))
    w = lambda k, n: _resident((None, k, n), lambda i: (l, 0, 0))
    return pl.pallas_call(
        _merge_kernel,
        grid=(t // tm,),
        in_specs=[row(D_MODEL), row(D_A), row(C_B), row(D_V_C), w(1, D_MODEL), w(1, D_MODEL),
                  w(D_MODEL, D_V_C), w(D_MODEL, N_BRANCH * D_MODEL), w(D_A, D_MODEL), w(C_B, D_MODEL),
                  w(D_V_C, D_MODEL), w(D_MODEL, D_MODEL)],
        out_specs=row(D_MODEL),
        out_shape=jax.ShapeDtypeStruct((t, D_MODEL), F32),
        compiler_params=_cparams(1),
        name="merge",
    )(x, oa, cb, on, gpre, gpost, wgc, wgate, wao, wpw, wro, wo)


FF_CHUNK = 1024


def _ffn_kernel(x_ref, gpre_ref, gpost_ref, wup_ref, wdn_ref, y_ref):
    x = x_ref[...]
    h = _rms(x, gpre_ref[...]).astype(BF16)
    ff = jnp.zeros(x.shape, F32)
    for c in range(D_FF // FF_CHUNK):
        up = jnp.maximum(_dot(h, wup_ref[:, c * FF_CHUNK:(c + 1) * FF_CHUNK]), 0.0)
        ff = ff + _dot((up * up).astype(BF16), wdn_ref[c * FF_CHUNK:(c + 1) * FF_CHUNK, :])
    y_ref[...] = x + _rms(ff, gpost_ref[...])


def _ffn(x, l, gpre, gpost, wup, wdn, tm):
    t = x.shape[0]
    row = pl.BlockSpec((tm, D_MODEL), lambda i: (i, 0))
    w = lambda k, n: _resident((None, k, n), lambda i: (l, 0, 0))
    return pl.pallas_call(
        _ffn_kernel,
        grid=(t // tm,),
        in_specs=[row, w(1, D_MODEL), w(1, D_MODEL), w(D_MODEL, D_FF), w(D_FF, D_MODEL)],
        out_specs=row,
        out_shape=jax.ShapeDtypeStruct((t, D_MODEL), F32),
        compiler_params=_cparams(1),
        name="ffn",
    )(x, gpre, gpost, wup, wdn)


def _rope_tables(pos):
    half = DK_C // 2
    inv = ROPE_BASE ** (-jnp.arange(half, dtype=F32) / half)
    ang = pos.astype(F32)[:, None] * inv[None, :]
    cos, sin = jnp.cos(ang), jnp.sin(ang)
    return jnp.concatenate([cos, cos], axis=1), jnp.concatenate([-sin, sin], axis=1)


def kernel(x_prompt, x_sample, cache_k, cache_v, cache_logf, state_conv, state_ret, page_table,
           meta_tokens, norm_mix_pre, norm_mix_post, norm_ffn_pre, norm_ffn_post,
           w_in, b_forget, w_dw, b_dw, ln_conv_g, ln_conv_b, w_pw_out, w_attn_out,
           gn_ret_g, w_ret_out, w_o, w_ff_up, w_ff_down):
    depth = w_in.shape[0]
    bp, seq_p, _ = x_prompt.shape
    n_dec, n_tok, _ = x_sample.shape
    n_pages = page_table.shape[1]
    n_pool, page = cache_k.shape[1], cache_k.shape[2]
    assert page == LANES
    past_len = n_pages * page
    lp = N_META + seq_p
    tp, ts = bp * lp, n_dec * n_tok

    cuts = np.cumsum([0, D_A, D_A, D_A, H_A, 2 * C_B, D_QK_C, D_QK_C, D_V_C, D_V_C, N_BRANCH * D_MODEL])
    seg = lambda a, b: w_in[:, :, int(cuts[a]):int(cuts[b])].astype(BF16)
    wa, wglu, wqk, wv, wgc, wgate = seg(0, 3), seg(4, 5), seg(5, 7), seg(7, 8), seg(8, 9), seg(9, 10)
    wf = jnp.pad(w_in[:, :, int(cuts[3]):int(cuts[4])], ((0, 0), (0, 0), (0, LANES - H_A))).astype(BF16)
    bfp = jnp.pad(b_forget, ((0, 0), (0, LANES - H_A)))[:, None, :]
    wao, wpw, wro, wo = (w.astype(BF16) for w in (w_attn_out, w_pw_out, w_ret_out, w_o))
    wup, wdn = w_ff_up.astype(BF16), w_ff_down.astype(BF16)
    vec = lambda a: a[:, None, :]
    g_mpre, g_mpost, g_fpre, g_fpost = vec(norm_mix_pre), vec(norm_mix_post), vec(norm_ffn_pre), vec(norm_ffn_post)
    wdw = jnp.pad(w_dw, ((0, 0), (0, CONV_PAD - CONV_W), (0, 0)))
    bdw, lng, lnb, gng = vec(b_dw), vec(ln_conv_g), vec(ln_conv_b), vec(gn_ret_g)

    tm_in_p = _row_tile(tp, 400, must_divide=lp)
    tm_in_s = _row_tile(ts, 256)
    tm_p = _row_tile(tp, 400)
    tm_s = _row_tile(ts, 512)
    cos_p, sin_p = _rope_tables(jnp.arange(lp))
    cos_s, sin_s = _rope_tables(past_len + jnp.arange(n_tok))
    cos_s, sin_s = (jnp.tile(a, (tm_in_s // n_tok, 1)) for a in (cos_s, sin_s))
    sel = _fox_aug_select()

    tri_u = jnp.asarray(np.triu(np.ones((LANES, LANES), np.float32)), dtype=BF16)
    cache_kt = cache_k.transpose(0, 1, 3, 4, 2)
    cache_vt = cache_v.transpose(0, 1, 3, 4, 2)
    lf_t = cache_logf.transpose(0, 1, 3, 2).reshape(depth * n_pool * H_A, page)
    cpool = _lane_cumsum(lf_t, tri_u).reshape(depth, n_pool, H_A, page)
    page_flat = page_table.reshape(-1)

    xp = jnp.concatenate([jnp.broadcast_to(meta_tokens[None], (bp, N_META, D_MODEL)), x_prompt], axis=1)
    xp = xp.reshape(tp, D_MODEL)
    xs = x_sample.reshape(ts, D_MODEL)
    buf0 = jnp.zeros((bp, CONV_W - 1, C_B), F32)
    nq = H_A * n_tok
    assert nq <= LANES
    own_head = jnp.asarray(np.arange(nq)[:, None] // n_tok == np.arange(D_A)[None, :] // DH_A)
    lane_grp = np.arange(LANES)
    t_grp = jnp.asarray(((lane_grp[:, None] // n_tok == lane_grp[None, :] // n_tok)
                         & (lane_grp[:, None] <= lane_grp[None, :])).astype(np.float32), dtype=BF16)

    def head_major(a):
        a = jnp.broadcast_to(a.reshape(n_dec, 1, n_tok, D_A), (n_dec, H_A, n_tok, D_A)).reshape(n_dec, nq, D_A)
        return jnp.where(own_head, a, 0).astype(BF16)

    outs = [[] for _ in range(10)]
    for l in range(depth):
        qa, ka, va, lf, u, qr, kr, vc = _inproj(xp, l, g_mpre, wa, wf, bfp, wglu, wqk, wv, cos_p, sin_p, tm_in_p)
        r3 = lambda a: a.reshape(bp, lp, a.shape[-1])
        oa = _fox_prompt(r3(qa), r3(ka), r3(va), r3(lf), sel)
        cact, nbuf = _conv(r3(u), buf0, l, wdw, bdw, lng, lnb, 1)
        on, s_p = _ret_prompt(r3(qr), r3(kr), r3(vc), l, gng)
        xp = _merge(xp, oa.reshape(tp, D_A), cact.reshape(tp, C_B), on.reshape(tp, D_V_C), l,
                    g_mpre, g_mpost, wgc, wgate, wao, wpw, wro, wo, tm_p)
        xp = _ffn(xp, l, g_fpre, g_fpost, wup, wdn, tm_p)
        outs[0].append(ka.reshape(bp, lp, H_A, DH_A))
        outs[1].append(va.reshape(bp, lp, H_A, DH_A))
        outs[2].append(lf[:, :H_A].reshape(bp, lp, H_A))
        outs[3].append(nbuf)
        outs[4].append(s_p)

        qa, ka, va, lf, u, qr, kr, vc = _inproj(xs, l, g_mpre, wa, wf, bfp, wglu, wqk, wv, cos_s, sin_s, tm_in_s)
        s3 = lambda a: a.reshape(n_dec, n_tok, a.shape[-1])
        lf8 = lf[:, :H_A]
        lf_new = jnp.pad(lf8.reshape(n_dec, n_tok, H_A).transpose(0, 2, 1).reshape(n_dec, 1, nq),
                         ((0, 0), (0, 0), (0, LANES - nq)))
        o_hq = _fox_sample(l, page_flat, head_major(qa), cache_kt, cache_vt, cpool,
                           head_major(ka), head_major(va), lf_new, t_grp, n_pages)
        oa = o_hq.reshape(n_dec, H_A, n_tok, DH_A).transpose(0, 2, 1, 3).reshape(ts, D_A)
        cact, nbuf = _conv(s3(u), state_conv, l, wdw, bdw, lng, lnb, 16)
        on, s_s = _ret_sample(s3(qr), s3(kr), s3(vc), l, gng, state_ret, 8)
        xs = _merge(xs, oa, cact.reshape(ts, C_B), on.reshape(ts, D_V_C), l,
                    g_mpre, g_mpost, wgc, wgate, wao, wpw, wro, wo, tm_s)
        xs = _ffn(xs, l, g_fpre, g_fpost, wup, wdn, tm_s)
        outs[5].append(ka.reshape(n_dec, n_tok, H_A, DH_A))
        outs[6].append(va.reshape(n_dec, n_tok, H_A, DH_A))
        outs[7].append(lf8.reshape(n_dec, n_tok, H_A))
        outs[8].append(nbuf)
        outs[9].append(s_s)

    y_prompt = xp.reshape(bp, lp, D_MODEL)[:, N_META:]
    y_sample = xs.reshape(n_dec, n_tok, D_MODEL)
    return (y_prompt, y_sample) + tuple(jnp.stack(o) for o in outs)
```

```python
import functools

import numpy as np
import jax
import jax.numpy as jnp
from jax import lax
from jax.experimental import pallas as pl
from jax.experimental.pallas import tpu as pltpu

D_MODEL = 1024
N_META = 16
H_A = 8
DH_A = 64
D_A = H_A * DH_A
C_B = 512
CONV_W = 31
H_C = 4
DK_C = 128
DV_C = 256
D_QK_C = H_C * DK_C
D_V_C = H_C * DV_C
N_BRANCH = 3
D_FF = 4 * D_MODEL
EPS = 1e-6
ROPE_BASE = 10000.0

LANES = 128
SUBLANES = 8
ATT_BLK = 256
RET_BLK = 128
NEG = -1e30
VMEM_LIMIT = 56 * 1024 * 1024
PAGES_PER_STEP = 16

BF16 = jnp.bfloat16
F32 = jnp.float32

_LOG_GAMMA = tuple(float(np.log1p(-(2.0 ** (-5.0 - h)))) for h in range(H_C))


def _cparams(n_axes):
    return pltpu.CompilerParams(dimension_semantics=("arbitrary",) * n_axes,
                                vmem_limit_bytes=VMEM_LIMIT)


def _row_tile(total, target, must_divide=None):
    best = None
    for t in range(SUBLANES, min(total, target) + 1, SUBLANES):
        if total % t == 0 and (must_divide is None or must_divide % t == 0):
            best = t
    assert best is not None, (total, target, must_divide)
    return best


def _resident(shape, index_map):
    return pl.BlockSpec(shape, index_map, pipeline_mode=pl.Buffered(1))


def _dot(a, b):
    return jnp.dot(a, b, preferred_element_type=F32)


def _dot_nt(a, b):
    return lax.dot_general(a, b, (((1,), (1,)), ((), ())), preferred_element_type=F32)


def _split3(x):
    hi = x.astype(BF16)
    r1 = x - hi.astype(F32)
    mid = r1.astype(BF16)
    r2 = r1 - mid.astype(F32)
    return hi, mid, r2.astype(BF16)


def _rms(x, g):
    return x * lax.rsqrt(jnp.mean(x * x, axis=-1, keepdims=True) + EPS) * g


def _log_sigmoid(z):
    return jnp.minimum(z, 0.0) - jnp.log1p(jnp.exp(-jnp.abs(z)))


def _sigmoid(z):
    return 1.0 / (1.0 + jnp.exp(-z))


def _inproj_kernel(x_ref, g_ref, wa_ref, wf_ref, bf_ref, wglu_ref, wqk_ref, wv_ref, cos_ref, sin_ref,
                   qa_ref, ka_ref, va_ref, lf_ref, u_ref, qr_ref, kr_ref, vc_ref):
    h = _rms(x_ref[...], g_ref[...]).astype(BF16)
    za = _dot(h, wa_ref[...])
    qa_ref[...] = (za[:, :D_A] * (DH_A ** -0.5)).astype(BF16)
    ka_ref[...] = za[:, D_A:2 * D_A]
    va_ref[...] = za[:, 2 * D_A:]
    lf_ref[...] = _log_sigmoid(_dot(h, wf_ref[...]) + bf_ref[...])
    zg = _dot(h, wglu_ref[...])
    u_ref[...] = zg[:, :C_B] * _sigmoid(zg[:, C_B:])
    zqk = _dot(h, wqk_ref[...])
    cos = cos_ref[...]
    sin = sin_ref[...]
    for j in range(2 * H_C):
        xh = zqk[:, j * DK_C:(j + 1) * DK_C]
        r = xh * cos + pltpu.roll(xh, DK_C // 2, 1) * sin
        if j < H_C:
            qr_ref[:, j * DK_C:(j + 1) * DK_C] = r.astype(BF16)
        else:
            kr_ref[:, (j - H_C) * DK_C:(j - H_C + 1) * DK_C] = r * (DK_C ** -0.5)
    vc_ref[...] = _dot(h, wv_ref[...]).astype(BF16)


def _inproj(x, l, gpre, wa, wf, bfp, wglu, wqk, wv, cos_t, sin_t, tm):
    t = x.shape[0]
    n_tab = cos_t.shape[0] // tm
    row = lambda n: pl.BlockSpec((tm, n), lambda i: (i, 0))
    wspec = lambda n: _resident((None, D_MODEL, n), lambda i: (l, 0, 0))
    tab = pl.BlockSpec((tm, LANES), lambda i: (i % n_tab, 0))
    outs = [(D_A, BF16), (D_A, F32), (D_A, F32), (LANES, F32), (C_B, F32),
            (D_QK_C, BF16), (D_QK_C, F32), (D_V_C, BF16)]
    return pl.pallas_call(
        _inproj_kernel,
        grid=(t // tm,),
        in_specs=[row(D_MODEL), _resident((None, 1, D_MODEL), lambda i: (l, 0, 0)),
                  wspec(3 * D_A), wspec(LANES), _resident((None, 1, LANES), lambda i: (l, 0, 0)),
                  wspec(2 * C_B), wspec(2 * D_QK_C), wspec(D_V_C), tab, tab],
        out_specs=[row(n) for n, _ in outs],
        out_shape=[jax.ShapeDtypeStruct((t, n), d) for n, d in outs],
        compiler_params=_cparams(1),
        name="inproj",
    )(x, gpre, wa, wf, bfp, wglu, wqk, wv, cos_t, sin_t)


N_AUG = 3
PACK_ONE = N_AUG * H_A


def _fox_aug_select():
    sel = np.zeros((H_A // 2, LANES, 4 * LANES), np.float32)
    for p in range(H_A // 2):
        for hh in range(2):
            h, x0 = 2 * p + hh, (1 - hh) * DH_A
            qb, kb = hh * LANES + x0, 2 * LANES + hh * LANES + x0
            for j in range(N_AUG):
                sel[p, j * H_A + h, qb + j] = 1.0
                sel[p, PACK_ONE, qb + N_AUG + j] = 1.0
                sel[p, PACK_ONE, kb + j] = 1.0
                sel[p, j * H_A + h, kb + N_AUG + j] = -1.0
    return jnp.asarray(sel, dtype=BF16)


def _fox_prompt_kernel(q_ref, k_ref, v_ref, lf_ref, sel_ref, o_ref, c_scr, qaug, kaug, vaug, *, seq):
    n_full = seq // LANES
    tail = seq - n_full * LANES
    n_blk = (seq - N_META) // ATT_BLK

    r = lax.broadcasted_iota(jnp.int32, (LANES, LANES), 0)
    c = lax.broadcasted_iota(jnp.int32, (LANES, LANES), 1)
    tri = jnp.where(c <= r, 1.0, 0.0).astype(BF16)
    carry = jnp.zeros((1, LANES), F32)
    for b in range(n_full + (1 if tail else 0)):
        nb = LANES if b < n_full else tail
        hi, mid, lo = _split3(lf_ref[b * LANES:b * LANES + nb, :])
        tb = tri[:nb, :nb]
        cb = _dot(tb, hi) + _dot(tb, mid) + _dot(tb, lo) + carry
        c_scr[b * LANES:b * LANES + nb, :] = cb
        carry = cb[nb - 1:nb, :]

    lane = lax.broadcasted_iota(jnp.int32, (seq, LANES), 1)
    cc = c_scr[...]
    hi = cc.astype(BF16).astype(F32)
    r1 = cc - hi
    mid = r1.astype(BF16).astype(F32)
    lo = r1 - mid
    packed = jnp.where(lane < H_A, hi,
                       jnp.where(lane < 2 * H_A, pltpu.roll(mid, H_A, 1),
                                 jnp.where(lane < PACK_ONE, pltpu.roll(lo, 2 * H_A, 1),
                                           jnp.where(lane == PACK_ONE, 1.0, 0.0)))).astype(BF16)
    for p in range(H_A // 2):
        aug = _dot(packed, sel_ref[p])
        qp = q_ref[:, p * LANES:(p + 1) * LANES].astype(F32)
        kp = k_ref[:, p * LANES:(p + 1) * LANES]
        vp = v_ref[:, p * LANES:(p + 1) * LANES]
        for hh in range(2):
            h, x0 = 2 * p + hh, (1 - hh) * DH_A
            own = (lane >= hh * DH_A) & (lane < (hh + 1) * DH_A)
            qaug[h] = jnp.where(own, qp, aug[:, hh * LANES:(hh + 1) * LANES]).astype(BF16)
            kaug[h] = jnp.where(own, kp, aug[:, (2 + hh) * LANES:(3 + hh) * LANES]).astype(BF16)
            vaug[h] = jnp.where(own, vp, jnp.where(lane == x0, 1.0, 0.0)).astype(BF16)

    def first(h, qa, nq):
        s = _dot_nt(qa, kaug[h, 0:N_META, :])
        if nq == N_META:
            rr = lax.broadcasted_iota(jnp.int32, s.shape, 0)
            cc2 = lax.broadcasted_iota(jnp.int32, s.shape, 1)
            s = jnp.where(rr >= cc2, s, NEG)
        m = jnp.max(s, axis=1, keepdims=True)
        acc = _dot(jnp.exp(s - m).astype(BF16), vaug[h, 0:N_META, :])
        return m, acc

    def steps(q0, start, carry, causal):
        qs = [qaug[h, pl.ds(q0, ATT_BLK), :] for h in range(H_A)]
        ks = [kaug[h, pl.ds(start, ATT_BLK), :] for h in range(H_A)]
        vs = [vaug[h, pl.ds(start, ATT_BLK), :] for h in range(H_A)]
        new = []
        for h in range(H_A):
            m_old, acc = carry[h]
            s = _dot_nt(qs[h], ks[h])
            if causal:
                rr = lax.broadcasted_iota(jnp.int32, s.shape, 0)
                cc2 = lax.broadcasted_iota(jnp.int32, s.shape, 1)
                s = jnp.where(rr >= cc2, s, NEG)
            m_new = jnp.maximum(m_old, jnp.max(s, axis=1, keepdims=True))
            pr = jnp.exp(s - m_new).astype(BF16)
            new.append((m_new, jnp.exp(m_old - m_new) * acc + _dot(pr, vs[h])))
        return tuple(new)

    def normalised(acc0, acc1, nq):
        ln = lax.broadcasted_iota(jnp.int32, (nq, LANES), 1)
        return jnp.where(ln < DH_A, acc0 / acc0[:, DH_A:DH_A + 1], acc1 / acc1[:, 0:1])

    for p in range(H_A // 2):
        a0 = first(2 * p, qaug[2 * p, 0:N_META, :], N_META)[1]
        a1 = first(2 * p + 1, qaug[2 * p + 1, 0:N_META, :], N_META)[1]
        o_ref[0:N_META, p * LANES:(p + 1) * LANES] = normalised(a0, a1, N_META).astype(o_ref.dtype)

    def q_block(i, _):
        q0 = pl.multiple_of(N_META + i * ATT_BLK, N_META)
        carry = tuple(first(h, qaug[h, pl.ds(q0, ATT_BLK), :], ATT_BLK) for h in range(H_A))

        def kv_block(j, carry):
            return steps(q0, pl.multiple_of(N_META + j * ATT_BLK, N_META), carry, False)

        carry = lax.fori_loop(0, i, kv_block, carry)
        carry = steps(q0, q0, carry, True)
        for p in range(H_A // 2):
            o_ref[pl.ds(q0, ATT_BLK), p * LANES:(p + 1) * LANES] = normalised(
                carry[2 * p][1], carry[2 * p + 1][1], ATT_BLK).astype(o_ref.dtype)
        return 0

    lax.fori_loop(0, n_blk, q_block, 0)


def _fox_prompt(qa, ka, va, lf, sel):
    b, seq, _ = qa.shape
    assert (seq - N_META) % ATT_BLK == 0
    full = lambda n: pl.BlockSpec((None, seq, n), lambda i: (i, 0, 0))
    heads = lambda: pltpu.VMEM((H_A, seq, LANES), BF16)
    return pl.pallas_call(
        functools.partial(_fox_prompt_kernel, seq=seq),
        grid=(b,),
        in_specs=[full(D_A), full(D_A), full(D_A), full(LANES),
                  _resident(sel.shape, lambda i: (0, 0, 0))],
        out_specs=full(D_A),
        out_shape=jax.ShapeDtypeStruct((b, seq, D_A), BF16),
        scratch_shapes=[pltpu.VMEM((seq, LANES), F32), heads(), heads(), heads()],
        compiler_params=_cparams(1),
        name="fox_prompt",
    )(qa, ka, va, lf, sel)


def _lane_cumsum_kernel(lf_ref, t_ref, c_ref):
    hi, mid, lo = _split3(lf_ref[...])
    t = t_ref[...]
    c_ref[...] = _dot(hi, t) + _dot(mid, t) + _dot(lo, t)


def _lane_cumsum(x, tri_u):
    rows = x.shape[0]
    tm = _row_tile(rows, 4096)
    spec = pl.BlockSpec((tm, LANES), lambda i: (i, 0))
    return pl.pallas_call(
        _lane_cumsum_kernel,
        grid=(rows // tm,),
        in_specs=[spec, _resident((LANES, LANES), lambda i: (0, 0))],
        out_specs=spec,
        out_shape=jax.ShapeDtypeStruct((rows, LANES), F32),
        compiler_params=_cparams(1),
        name="lane_cumsum",
    )(x, tri_u)


def _fox_sample_kernel(pt_ref, q_ref, *refs, n_steps, g_pages, n_tok):
    del pt_ref
    k_refs, v_refs, c_refs = refs[:g_pages], refs[g_pages:2 * g_pages], refs[2 * g_pages:3 * g_pages]
    kn_ref, vn_ref, lfn_ref, t_ref, o_ref, m_scr, l_scr, acc_scr, carry_scr = refs[3 * g_pages:]
    p = pl.program_id(1)
    page = k_refs[0].shape[-1]
    nq = q_ref.shape[0]
    q = q_ref[...]

    def update(s, pv):
        m_old = m_scr[...]
        m_new = jnp.maximum(m_old, jnp.max(s, axis=1, keepdims=True))
        alpha = jnp.exp(m_old - m_new)
        pr = jnp.exp(s - m_new)
        m_scr[...] = m_new
        l_scr[...] = alpha * l_scr[...] + jnp.sum(pr, axis=1, keepdims=True)
        acc_scr[...] = alpha * acc_scr[...] + pv(pr.astype(BF16))

    @pl.when(p == 0)
    def _():
        m_scr[...] = jnp.full(m_scr.shape, NEG, F32)
        l_scr[...] = jnp.zeros(l_scr.shape, F32)
        acc_scr[...] = jnp.zeros(acc_scr.shape, F32)
        carry_scr[...] = jnp.zeros(carry_scr.shape, F32)
        hi, mid, lo = _split3(jnp.broadcast_to(lfn_ref[...], (SUBLANES, LANES)))
        t = t_ref[...]
        c_new = (_dot(hi, t) + _dot(mid, t) + _dot(lo, t))[0:1, :nq]
        sn = _dot_nt(q, kn_ref[...]) - c_new
        rr = lax.broadcasted_iota(jnp.int32, sn.shape, 0)
        cc = lax.broadcasted_iota(jnp.int32, sn.shape, 1)
        ok = (rr // n_tok == cc // n_tok) & (cc % n_tok <= rr % n_tok)
        update(jnp.where(ok, sn, NEG), lambda pb: _dot(pb, vn_ref[...]))

    def per_head_rows(x):
        return jnp.concatenate([jnp.broadcast_to(x[h:h + 1, :], (n_tok, x.shape[1])) for h in range(H_A)], axis=0)

    carry = carry_scr[...]
    bias, ks, vs = [], [], []
    for g in range(g_pages):
        cin = c_refs[g][...]
        tot = cin[:, page - 1:page]
        bias.append(per_head_rows((tot - cin) + carry))
        carry = carry + tot
        ks.append(k_refs[g][...].reshape(D_A, page).astype(BF16))
        vs.append(v_refs[g][...].reshape(D_A, page).astype(BF16))
    carry_scr[...] = carry
    v_all = jnp.concatenate(vs, axis=1)
    s = _dot(q, jnp.concatenate(ks, axis=1)) + jnp.concatenate(bias, axis=1)
    update(s, lambda pb: _dot_nt(pb, v_all))

    @pl.when(p == n_steps - 1)
    def _():
        acc = acc_scr[...] / l_scr[...]
        o_ref[...] = jnp.concatenate(
            [acc[h * n_tok:(h + 1) * n_tok, h * DH_A:(h + 1) * DH_A] for h in range(H_A)], axis=0).astype(o_ref.dtype)


def _fox_sample(l, page_flat, q_hq, cache_kt, cache_vt, cpool, k_new, v_new, lf_new, tri_u, n_pages):
    n_seq, nq, _ = q_hq.shape
    page = cache_kt.shape[-1]
    g_pages = max(g for g in range(1, PAGES_PER_STEP + 1) if n_pages % g == 0)
    n_steps = n_pages // g_pages
    slot = lambda s, p, g: s * n_pages + (n_pages - 1 - (p * g_pages + g))
    idx = lambda g: (lambda s, p, pt: (l, pt[slot(s, p, g)], 0, 0, 0))
    cidx = lambda g: (lambda s, p, pt: (l, pt[slot(s, p, g)], 0, 0))
    kv_specs = [pl.BlockSpec((None, None, H_A, DH_A, page), idx(g)) for g in range(g_pages)]
    c_specs = [pl.BlockSpec((None, None, H_A, page), cidx(g)) for g in range(g_pages)]
    seq_spec = lambda a, b: pl.BlockSpec((None, a, b), lambda s, p, pt: (s, 0, 0))
    grid_spec = pltpu.PrefetchScalarGridSpec(
        num_scalar_prefetch=1,
        grid=(n_seq, n_steps),
        in_specs=[seq_spec(nq, D_A)] + kv_specs + kv_specs + c_specs
                 + [seq_spec(nq, D_A), seq_spec(nq, D_A), seq_spec(1, LANES),
                    pl.BlockSpec((LANES, LANES), lambda s, p, pt: (0, 0))],
        out_specs=seq_spec(nq, DH_A),
        scratch_shapes=[pltpu.VMEM((nq, 1), F32), pltpu.VMEM((nq, 1), F32),
                        pltpu.VMEM((nq, D_A), F32), pltpu.VMEM((H_A, 1), F32)],
    )
    return pl.pallas_call(
        functools.partial(_fox_sample_kernel, n_steps=n_steps, g_pages=g_pages, n_tok=nq // H_A),
        grid_spec=grid_spec,
        out_shape=jax.ShapeDtypeStruct((n_seq, nq, DH_A), BF16),
        compiler_params=_cparams(2),
        name="fox_sample",
    )(page_flat, q_hq, *([cache_kt] * g_pages), *([cache_vt] * g_pages), *([cpool] * g_pages),
      k_new, v_new, lf_new, tri_u)


CONV_PAD = 32


def _conv_kernel(u_ref, buf_ref, w_ref, b_ref, g_ref, beta_ref, act_ref, nbuf_ref, ext, dw, *, nb, seq, tr):
    n_tiles = -(-seq // tr)
    keep = CONV_W - 1
    for s in range(nb):
        ext[s, 0:CONV_PAD - keep, :] = jnp.zeros((CONV_PAD - keep, C_B), F32)
        ext[s, CONV_PAD - keep:CONV_PAD, :] = buf_ref[s]
        ext[s, CONV_PAD:CONV_PAD + seq, :] = u_ref[s]
        if n_tiles * tr > seq:
            ext[s, CONV_PAD + seq:CONV_PAD + n_tiles * tr, :] = jnp.zeros((n_tiles * tr - seq, C_B), F32)
        nbuf_ref[s] = ext[s, CONV_PAD + seq - keep:CONV_PAD + seq, :]

        def tile(i, _, s=s):
            r0 = pl.multiple_of(i * tr, SUBLANES)
            for c in range(C_B // LANES):
                win = ext[s, pl.ds(r0, tr + CONV_PAD), c * LANES:(c + 1) * LANES]
                shifted = [win] + [pltpu.roll(win, tr + CONV_PAD - sh, 0) for sh in range(1, SUBLANES)]
                acc = jnp.broadcast_to(b_ref[:, c * LANES:(c + 1) * LANES], (tr, LANES))
                for j in range(CONV_W):
                    a, sh = divmod(CONV_PAD - keep + j, SUBLANES)
                    acc = acc + shifted[sh][a * SUBLANES:a * SUBLANES + tr, :] * w_ref[j:j + 1, c * LANES:(c + 1) * LANES]
                dw[pl.ds(r0, tr), c * LANES:(c + 1) * LANES] = acc
            return 0

        lax.fori_loop(0, n_tiles, tile, 0)

        def norm(r0, n, s=s):
            x = dw[pl.ds(r0, n), :]
            mu = jnp.mean(x, axis=-1, keepdims=True)
            xc = x - mu
            var = jnp.mean(xc * xc, axis=-1, keepdims=True)
            y = xc * lax.rsqrt(var + EPS) * g_ref[...] + beta_ref[...]
            act_ref[s, pl.ds(r0, n), :] = (y * _sigmoid(y)).astype(act_ref.dtype)

        n_full = seq // tr

        def norm_tile(i, _):
            norm(pl.multiple_of(i * tr, SUBLANES), tr)
            return 0

        lax.fori_loop(0, n_full, norm_tile, 0)
        if seq > n_full * tr:
            norm(n_full * tr, seq - n_full * tr)


def _conv(u, buf, l, w_dw, b_dw, ln_g, ln_b, nb):
    bsz, seq, _ = u.shape
    tr = LANES if seq >= LANES else seq
    n_tiles = -(-seq // tr)
    keep = CONV_W - 1
    par = lambda r: pl.BlockSpec((None, r, C_B), lambda i: (l, 0, 0))
    if buf.ndim == 4:
        buf_spec = pl.BlockSpec((None, nb, keep, C_B), lambda i: (l, i, 0, 0))
    else:
        buf_spec = pl.BlockSpec((nb, keep, C_B), lambda i: (i, 0, 0))
    return pl.pallas_call(
        functools.partial(_conv_kernel, nb=nb, seq=seq, tr=tr),
        grid=(bsz // nb,),
        in_specs=[pl.BlockSpec((nb, seq, C_B), lambda i: (i, 0, 0)), buf_spec,
                  par(CONV_PAD), par(1), par(1), par(1)],
        out_specs=[pl.BlockSpec((nb, seq, C_B), lambda i: (i, 0, 0)),
                   pl.BlockSpec((nb, keep, C_B), lambda i: (i, 0, 0))],
        out_shape=[jax.ShapeDtypeStruct((bsz, seq, C_B), BF16),
                   jax.ShapeDtypeStruct((bsz, keep, C_B), F32)],
        scratch_shapes=[pltpu.VMEM((nb, CONV_PAD + n_tiles * tr, C_B), F32),
                        pltpu.VMEM((n_tiles * tr, C_B), F32)],
        compiler_params=_cparams(1),
        name="conv",
    )(u, buf, w_dw, b_dw, ln_g, ln_b)


def _ret_chunk(s, q, k, v, n, lg):
    ii = lax.broadcasted_iota(jnp.int32, (n, n), 0)
    jj = lax.broadcasted_iota(jnp.int32, (n, n), 1)
    rel = (ii - jj).astype(F32)
    decay = jnp.where(rel >= 0, jnp.exp(jnp.maximum(rel, 0.0) * lg), 0.0)
    scores = _dot_nt(q, k.astype(BF16)) * decay
    o = _dot(scores.astype(BF16), v)
    jc = lax.broadcasted_iota(jnp.int32, (n, 1), 0).astype(F32)
    o = o + _dot(q, s.astype(BF16)) * jnp.exp((jc + 1.0) * lg)
    ks = (k * jnp.exp((n - 1.0 - jc) * lg)).astype(BF16)
    kv = lax.dot_general(ks, v, (((0,), (0,)), ((), ())), preferred_element_type=F32)
    gamma_n = float(np.exp(n * lg)) if isinstance(lg, float) else jnp.exp(n * lg)
    return gamma_n * s + kv, o


def _group_norm(o, g):
    mu = jnp.mean(o, axis=-1, keepdims=True)
    oc = o - mu
    var = jnp.mean(oc * oc, axis=-1, keepdims=True)
    return oc * lax.rsqrt(var + EPS) * g


def _head_log_gamma(h):
    hv = jnp.full((1, 1), h, jnp.int32)
    lg = jnp.full((1, 1), _LOG_GAMMA[H_C - 1], F32)
    for i in range(H_C - 2, -1, -1):
        lg = jnp.where(hv == i, jnp.float32(_LOG_GAMMA[i]), lg)
    return lg


def _ret_prompt_kernel(q_ref, k_ref, v_ref, g_ref, o_ref, s_ref, *, seq):
    def chunk(r0, n, states):
        qs = [q_ref[pl.ds(r0, n), h * DK_C:(h + 1) * DK_C] for h in range(H_C)]
        ks = [k_ref[pl.ds(r0, n), h * DK_C:(h + 1) * DK_C] for h in range(H_C)]
        vs = [v_ref[pl.ds(r0, n), h * DV_C:(h + 1) * DV_C] for h in range(H_C)]
        new, outs = [], []
        for h in range(H_C):
            s, o = _ret_chunk(states[h], qs[h], ks[h], vs[h], n, _LOG_GAMMA[h])
            new.append(s)
            outs.append(_group_norm(o, g_ref[:, h * DV_C:(h + 1) * DV_C]))
        o_ref[pl.ds(r0, n), :] = jnp.concatenate(outs, axis=1).astype(o_ref.dtype)
        return tuple(new)

    states = chunk(0, N_META, tuple(jnp.zeros((DK_C, DV_C), F32) for _ in range(H_C)))
    states = lax.fori_loop(0, (seq - N_META) // RET_BLK,
                           lambda i, st: chunk(pl.multiple_of(N_META + i * RET_BLK, N_META), RET_BLK, st), states)
    for h in range(H_C):
        s_ref[h] = states[h]


def _ret_prompt(qr, kr, vc, l, gn_g):
    b, seq, _ = qr.shape
    assert (seq - N_META) % RET_BLK == 0
    full = lambda n: pl.BlockSpec((None, seq, n), lambda i: (i, 0, 0))
    return pl.pallas_call(
        functools.partial(_ret_prompt_kernel, seq=seq),
        grid=(b,),
        in_specs=[full(D_QK_C), full(D_QK_C), full(D_V_C), pl.BlockSpec((None, 1, D_V_C), lambda i: (l, 0, 0))],
        out_specs=[full(D_V_C), pl.BlockSpec((None, H_C, DK_C, DV_C), lambda i: (i, 0, 0, 0))],
        out_shape=[jax.ShapeDtypeStruct((b, seq, D_V_C), BF16),
                   jax.ShapeDtypeStruct((b, H_C, DK_C, DV_C), F32)],
        compiler_params=_cparams(1),
        name="ret_prompt",
    )(qr, kr, vc, gn_g)


def _ret_sample_kernel(q_ref, k_ref, v_ref, g_ref, s0_ref, o_ref, s_ref, *, nb, n_tok):
    lg = _head_log_gamma(pl.program_id(1))
    g = g_ref[...]
    for s in range(nb):
        s_new, o = _ret_chunk(s0_ref[s], q_ref[s], k_ref[s], v_ref[s], n_tok, lg)
        o_ref[s] = _group_norm(o, g).astype(o_ref.dtype)
        s_ref[s] = s_new


def _ret_sample(qr, kr, vc, l, gn_g, state, nb):
    n_seq, n_tok, _ = qr.shape
    qk = pl.BlockSpec((nb, n_tok, DK_C), lambda i, h: (i, 0, h))
    vv = pl.BlockSpec((nb, n_tok, DV_C), lambda i, h: (i, 0, h))
    return pl.pallas_call(
        functools.partial(_ret_sample_kernel, nb=nb, n_tok=n_tok),
        grid=(n_seq // nb, H_C),
        in_specs=[qk, qk, vv, pl.BlockSpec((None, 1, DV_C), lambda i, h: (l, 0, h)),
                  pl.BlockSpec((None, nb, None, DK_C, DV_C), lambda i, h: (l, i, h, 0, 0))],
        out_specs=[vv, pl.BlockSpec((nb, None, DK_C, DV_C), lambda i, h: (i, h, 0, 0))],
        out_shape=[jax.ShapeDtypeStruct((n_seq, n_tok, D_V_C), BF16),
                   jax.ShapeDtypeStruct((n_seq, H_C, DK_C, DV_C), F32)],
        compiler_params=_cparams(2),
        name="ret_sample",
    )(qr, kr, vc, gn_g, state)


def _merge_kernel(x_ref, oa_ref, cb_ref, on_ref, gpre_ref, gpost_ref, wgc_ref, wgate_ref, wao_ref,
                  wpw_ref, wro_ref, wo_ref, y_ref):
    x = x_ref[...]
    h = _rms(x, gpre_ref[...]).astype(BF16)
    zc = _dot(h, wgc_ref[...])
    y_c = _dot((zc * _sigmoid(zc) * on_ref[...]).astype(BF16), wro_ref[...])
    y_a = _dot(oa_ref[...], wao_ref[...])
    y_b = _dot(cb_ref[...], wpw_ref[...])
    merged = (_sigmoid(_dot(h, wgate_ref[:, 0:D_MODEL])) * y_a
              + _sigmoid(_dot(h, wgate_ref[:, D_MODEL:2 * D_MODEL])) * y_b
              + _sigmoid(_dot(h, wgate_ref[:, 2 * D_MODEL:])) * y_c)
    out = _dot(merged.astype(BF16), wo_ref[...])
    y_ref[...] = x + _rms(out, gpost_ref[...])


def _merge(x, oa, cb, on, l, gpre, gpost, wgc, wgate, wao, wpw, wro, wo, tm):
    t = x.shape[0]
    row = lambda n: pl.BlockSpec((tm, n), lambda i: (i, 0))
    w = lambda k, n: _resident((None, k, n), lambda i: (l, 0, 0))
    return pl.pallas_call(
        _merge_kernel,
        grid=(t // tm,),
        in_specs=[row(D_MODEL), row(D_A), row(C_B), row(D_V_C), w(1, D_MODEL), w(1, D_MODEL),
                  w(D_MODEL, D_V_C), w(D_MODEL, N_BRANCH * D_MODEL), w(D_A, D_MODEL), w(C_B, D_MODEL),
                  w(D_V_C, D_MODEL), w(D_MODEL, D_MODEL)],
        out_specs=row(D_MODEL),
        out_shape=jax.ShapeDtypeStruct((t, D_MODEL), F32),
        compiler_params=_cparams(1),
        name="merge",
    )(x, oa, cb, on, gpre, gpost, wgc, wgate, wao, wpw, wro, wo)


FF_CHUNK = 1024


def _ffn_kernel(x_ref, gpre_ref, gpost_ref, wup_ref, wdn_ref, y_ref):
    x = x_ref[...]
    h = _rms(x, gpre_ref[...]).astype(BF16)
    ff = jnp.zeros(x.shape, F32)
    for c in range(D_FF // FF_CHUNK):
        up = jnp.maximum(_dot(h, wup_ref[:, c * FF_CHUNK:(c + 1) * FF_CHUNK]), 0.0)
        ff = ff + _dot((up * up).astype(BF16), wdn_ref[c * FF_CHUNK:(c + 1) * FF_CHUNK, :])
    y_ref[...] = x + _rms(ff, gpost_ref[...])


def _ffn(x, l, gpre, gpost, wup, wdn, tm):
    t = x.shape[0]
    row = pl.BlockSpec((tm, D_MODEL), lambda i: (i, 0))
    w = lambda k, n: _resident((None, k, n), lambda i: (l, 0, 0))
    return pl.pallas_call(
        _ffn_kernel,
        grid=(t // tm,),
        in_specs=[row, w(1, D_MODEL), w(1, D_MODEL), w(D_MODEL, D_FF), w(D_FF, D_MODEL)],
        out_specs=row,
        out_shape=jax.ShapeDtypeStruct((t, D_MODEL), F32),
        compiler_params=_cparams(1),
        name="ffn",
    )(x, gpre, gpost, wup, wdn)


def _rope_tables(pos):
    half = DK_C // 2
    inv = ROPE_BASE ** (-jnp.arange(half, dtype=F32) / half)
    ang = pos.astype(F32)[:, None] * inv[None, :]
    cos, sin = jnp.cos(ang), jnp.sin(ang)
    return jnp.concatenate([cos, cos], axis=1), jnp.concatenate([-sin, sin], axis=1)


def kernel(x_prompt, x_sample, cache_k, cache_v, cache_logf, state_conv, state_ret, page_table,
           meta_tokens, norm_mix_pre, norm_mix_post, norm_ffn_pre, norm_ffn_post,
           w_in, b_forget, w_dw, b_dw, ln_conv_g, ln_conv_b, w_pw_out, w_attn_out,
           gn_ret_g, w_ret_out, w_o, w_ff_up, w_ff_down):
    depth = w_in.shape[0]
    bp, seq_p, _ = x_prompt.shape
    n_dec, n_tok, _ = x_sample.shape
    n_pages = page_table.shape[1]
    n_pool, page = cache_k.shape[1], cache_k.shape[2]
    assert page == LANES
    past_len = n_pages * page
    lp = N_META + seq_p
    tp, ts = bp * lp, n_dec * n_tok

    cuts = np.cumsum([0, D_A, D_A, D_A, H_A, 2 * C_B, D_QK_C, D_QK_C, D_V_C, D_V_C, N_BRANCH * D_MODEL])
    seg = lambda a, b: w_in[:, :, int(cuts[a]):int(cuts[b])].astype(BF16)
    wa, wglu, wqk, wv, wgc, wgate = seg(0, 3), seg(4, 5), seg(5, 7), seg(7, 8), seg(8, 9), seg(9, 10)
    wf = jnp.pad(w_in[:, :, int(cuts[3]):int(cuts[4])], ((0, 0), (0, 0), (0, LANES - H_A))).astype(BF16)
    bfp = jnp.pad(b_forget, ((0, 0), (0, LANES - H_A)))[:, None, :]
    wao, wpw, wro, wo = (w.astype(BF16) for w in (w_attn_out, w_pw_out, w_ret_out, w_o))
    wup, wdn = w_ff_up.astype(BF16), w_ff_down.astype(BF16)
    vec = lambda a: a[:, None, :]
    g_mpre, g_mpost, g_fpre, g_fpost = vec(norm_mix_pre), vec(norm_mix_post), vec(norm_ffn_pre), vec(norm_ffn_post)
    wdw = jnp.pad(w_dw, ((0, 0), (0, CONV_PAD - CONV_W), (0, 0)))
    bdw, lng, lnb, gng = vec(b_dw), vec(ln_conv_g), vec(ln_conv_b), vec(gn_ret_g)

    tm_in_p = _row_tile(tp, 400, must_divide=lp)
    tm_in_s = _row_tile(ts, 256)
    tm_p = _row_tile(tp, 400)
    tm_s = _row_tile(ts, 512)
    cos_p, sin_p = _rope_tables(jnp.arange(lp))
    cos_s, sin_s = _rope_tables(past_len + jnp.arange(n_tok))
    cos_s, sin_s = (jnp.tile(a, (tm_in_s // n_tok, 1)) for a in (cos_s, sin_s))
    sel = _fox_aug_select()

    tri_u = jnp.asarray(np.triu(np.ones((LANES, LANES), np.float32)), dtype=BF16)
    cache_kt = cache_k.transpose(0, 1, 3, 4, 2)
    cache_vt = cache_v.transpose(0, 1, 3, 4, 2)
    lf_t = cache_logf.transpose(0, 1, 3, 2).reshape(depth * n_pool * H_A, page)
    cpool = _lane_cumsum(lf_t, tri_u).reshape(depth, n_pool, H_A, page)
    page_flat = page_table.reshape(-1)

    xp = jnp.concatenate([jnp.broadcast_to(meta_tokens[None], (bp, N_META, D_MODEL)), x_prompt], axis=1)
    xp = xp.reshape(tp, D_MODEL)
    xs = x_sample.reshape(ts, D_MODEL)
    buf0 = jnp.zeros((bp, CONV_W - 1, C_B), F32)
    nq = H_A * n_tok
    assert nq <= LANES
    own_head = jnp.asarray(np.arange(nq)[:, None] // n_tok == np.arange(D_A)[None, :] // DH_A)
    lane_grp = np.arange(LANES)
    t_grp = jnp.asarray(((lane_grp[:, None] // n_tok == lane_grp[None, :] // n_tok)
                         & (lane_grp[:, None] <= lane_grp[None, :])).astype(np.float32), dtype=BF16)

    def head_major(a):
        a = jnp.broadcast_to(a.reshape(n_dec, 1, n_tok, D_A), (n_dec, H_A, n_tok, D_A)).reshape(n_dec, nq, D_A)
        return jnp.where(own_head, a, 0).astype(BF16)

    outs = [[] for _ in range(10)]
    for l in range(depth):
        qa, ka, va, lf, u, qr, kr, vc = _inproj(xp, l, g_mpre, wa, wf, bfp, wglu, wqk, wv, cos_p, sin_p, tm_in_p)
        r3 = lambda a: a.reshape(bp, lp, a.shape[-1])
        oa = _fox_prompt(r3(qa), r3(ka), r3(va), r3(lf), sel)
        cact, nbuf = _conv(r3(u), buf0, l, wdw, bdw, lng, lnb, 1)
        on, s_p = _ret_prompt(r3(qr), r3(kr), r3(vc), l, gng)
        xp = _merge(xp, oa.reshape(tp, D_A), cact.reshape(tp, C_B), on.reshape(tp, D_V_C), l,
                    g_mpre, g_mpost, wgc, wgate, wao, wpw, wro, wo, tm_p)
        xp = _ffn(xp, l, g_fpre, g_fpost, wup, wdn, tm_p)
        outs[0].append(ka.reshape(bp, lp, H_A, DH_A))
        outs[1].append(va.reshape(bp, lp, H_A, DH_A))
        outs[2].append(lf[:, :H_A].reshape(bp, lp, H_A))
        outs[3].append(nbuf)
        outs[4].append(s_p)

        qa, ka, va, lf, u, qr, kr, vc = _inproj(xs, l, g_mpre, wa, wf, bfp, wglu, wqk, wv, cos_s, sin_s, tm_in_s)
        s3 = lambda a: a.reshape(n_dec, n_tok, a.shape[-1])
        lf8 = lf[:, :H_A]
        lf_new = jnp.pad(lf8.reshape(n_dec, n_tok, H_A).transpose(0, 2, 1).reshape(n_dec, 1, nq),
                         ((0, 0), (0, 0), (0, LANES - nq)))
        o_hq = _fox_sample(l, page_flat, head_major(qa), cache_kt, cache_vt, cpool,
                           head_major(ka), head_major(va), lf_new, t_grp, n_pages)
        oa = o_hq.reshape(n_dec, H_A, n_tok, DH_A).transpose(0, 2, 1, 3).reshape(ts, D_A)
        cact, nbuf = _conv(s3(u), state_conv, l, wdw, bdw, lng, lnb, 16)
        on, s_s = _ret_sample(s3(qr), s3(kr), s3(vc), l, gng, state_ret, 8)
        xs = _merge(xs, oa, cact.reshape(ts, C_B), on.reshape(ts, D_V_C), l,
                    g_mpre, g_mpost, wgc, wgate, wao, wpw, wro, wo, tm_s)
        xs = _ffn(xs, l, g_fpre, g_fpost, wup, wdn, tm_s)
        outs[5].append(ka.reshape(n_dec, n_tok, H_A, DH_A))
        outs[6].append(va.reshape(n_dec, n_tok, H_A, DH_A))
        outs[7].append(lf8.reshape(n_dec, n_tok, H_A))
        outs[8].append(nbuf)
        outs[9].append(s_s)

    y_prompt = xp.reshape(bp, lp, D_MODEL)[:, N_META:]
    y_sample = xs.reshape(n_dec, n_tok, D_MODEL)
    return (y_prompt, y_sample) + tuple(jnp.stack(o) for o in outs)
```

```python
import functools

import numpy as np
import jax
import jax.numpy as jnp
from jax import lax
from jax.experimental import pallas as pl
from jax.experimental.pallas import tpu as pltpu

D_MODEL = 1024
N_META = 16
H_A = 8
DH_A = 64
D_A = H_A * DH_A
C_B = 512
CONV_W = 31
H_C = 4
DK_C = 128
DV_C = 256
D_QK_C = H_C * DK_C
D_V_C = H_C * DV_C
N_BRANCH = 3
D_FF = 4 * D_MODEL
EPS = 1e-6
ROPE_BASE = 10000.0

LANES = 128
SUBLANES = 8
ATT_BLK = 512
RET_BLK = 128
NEG = -1e30
VMEM_LIMIT = 56 * 1024 * 1024
PAGES_PER_STEP = 16

BF16 = jnp.bfloat16
F32 = jnp.float32

_LOG_GAMMA = tuple(float(np.log1p(-(2.0 ** (-5.0 - h)))) for h in range(H_C))


def _cparams(n_axes):
    return pltpu.CompilerParams(dimension_semantics=("arbitrary",) * n_axes,
                                vmem_limit_bytes=VMEM_LIMIT)


def _row_tile(total, target, must_divide=None):
    best = None
    for t in range(SUBLANES, min(total, target) + 1, SUBLANES):
        if total % t == 0 and (must_divide is None or must_divide % t == 0):
            best = t
    assert best is not None, (total, target, must_divide)
    return best


def _resident(shape, index_map):
    return pl.BlockSpec(shape, index_map, pipeline_mode=pl.Buffered(1))


def _dot(a, b):
    return jnp.dot(a, b, preferred_element_type=F32)


def _dot_nt(a, b):
    return lax.dot_general(a, b, (((1,), (1,)), ((), ())), preferred_element_type=F32)


def _split3(x):
    hi = x.astype(BF16)
    r1 = x - hi.astype(F32)
    mid = r1.astype(BF16)
    r2 = r1 - mid.astype(F32)
    return hi, mid, r2.astype(BF16)


def _rms(x, g):
    return x * lax.rsqrt(jnp.mean(x * x, axis=-1, keepdims=True) + EPS) * g


def _log_sigmoid(z):
    return jnp.minimum(z, 0.0) - jnp.log1p(jnp.exp(-jnp.abs(z)))


def _sigmoid(z):
    return 1.0 / (1.0 + jnp.exp(-z))


def _stacked_out(shape, block, index_map, l, prev):
    spec = pl.BlockSpec((None,) + block, lambda *i: (l,) + index_map(*i))
    extra_in = [] if prev is None else [(pl.BlockSpec(memory_space=pl.ANY), prev)]
    return spec, jax.ShapeDtypeStruct(shape, F32), extra_in


def _inproj_kernel(x_ref, g_ref, wa_ref, wf_ref, bf_ref, wglu_ref, wqk_ref, wv_ref, cos_ref, sin_ref, *rest):
    qa_ref, ka_ref, va_ref, lf_ref, u_ref, qr_ref, kr_ref, vc_ref = rest[-8:]
    h = _rms(x_ref[...], g_ref[...]).astype(BF16)
    za = _dot(h, wa_ref[...])
    qa_ref[...] = (za[:, :D_A] * (DH_A ** -0.5)).astype(BF16)
    ka_ref[...] = za[:, D_A:2 * D_A]
    va_ref[...] = za[:, 2 * D_A:]
    lf_ref[...] = _log_sigmoid(_dot(h, wf_ref[...]) + bf_ref[...])
    zg = _dot(h, wglu_ref[...])
    u_ref[...] = zg[:, :C_B] * _sigmoid(zg[:, C_B:])
    zqk = _dot(h, wqk_ref[...])
    cos = cos_ref[...]
    sin = sin_ref[...]
    for j in range(2 * H_C):
        xh = zqk[:, j * DK_C:(j + 1) * DK_C]
        r = xh * cos + pltpu.roll(xh, DK_C // 2, 1) * sin
        if j < H_C:
            qr_ref[:, j * DK_C:(j + 1) * DK_C] = r.astype(BF16)
        else:
            kr_ref[:, (j - H_C) * DK_C:(j - H_C + 1) * DK_C] = r * (DK_C ** -0.5)
    vc_ref[...] = _dot(h, wv_ref[...]).astype(BF16)


def _inproj(x, l, gpre, wa, wf, bfp, wglu, wqk, wv, cos_t, sin_t, tm, k_prev, v_prev):
    t = x.shape[0]
    depth = wa.shape[0]
    n_tab = cos_t.shape[0] // tm
    row = lambda n: pl.BlockSpec((tm, n), lambda i: (i, 0))
    wspec = lambda n: _resident((None, D_MODEL, n), lambda i: (l, 0, 0))
    tab = pl.BlockSpec((tm, LANES), lambda i: (i % n_tab, 0))
    outs = [(D_A, BF16), None, None, (LANES, F32), (C_B, F32), (D_QK_C, BF16), (D_QK_C, F32), (D_V_C, BF16)]
    out_specs = [None if o is None else row(o[0]) for o in outs]
    out_shape = [None if o is None else jax.ShapeDtypeStruct((t, o[0]), o[1]) for o in outs]
    in_specs = [row(D_MODEL), _resident((None, 1, D_MODEL), lambda i: (l, 0, 0)),
                wspec(3 * D_A), wspec(LANES), _resident((None, 1, LANES), lambda i: (l, 0, 0)),
                wspec(2 * C_B), wspec(2 * D_QK_C), wspec(D_V_C), tab, tab]
    args = [x, gpre, wa, wf, bfp, wglu, wqk, wv, cos_t, sin_t]
    aliases = {}
    for o, prev in ((1, k_prev), (2, v_prev)):
        out_specs[o], out_shape[o], extra = _stacked_out((depth, t, D_A), (tm, D_A), lambda i: (i, 0), l, prev)
        for spec, arr in extra:
            aliases[len(args)] = o
            in_specs.append(spec)
            args.append(arr)
    return pl.pallas_call(
        _inproj_kernel,
        grid=(t // tm,),
        in_specs=in_specs,
        out_specs=out_specs,
        out_shape=out_shape,
        input_output_aliases=aliases,
        compiler_params=_cparams(1),
        name="inproj",
    )(*args)


N_AUG = 3
PACK_ONE = N_AUG * H_A


def _fox_aug_select():
    sel = np.zeros((H_A // 2, LANES, 4 * LANES), np.float32)
    for p in range(H_A // 2):
        for hh in range(2):
            h, x0 = 2 * p + hh, (1 - hh) * DH_A
            qb, kb = hh * LANES + x0, 2 * LANES + hh * LANES + x0
            for j in range(N_AUG):
                sel[p, j * H_A + h, qb + j] = 1.0
                sel[p, PACK_ONE, qb + N_AUG + j] = 1.0
                sel[p, PACK_ONE, kb + j] = 1.0
                sel[p, j * H_A + h, kb + N_AUG + j] = -1.0
    return jnp.asarray(sel, dtype=BF16)


def _fox_prompt_kernel(q_ref, k_ref, v_ref, lf_ref, sel_ref, o_ref, c_scr, qaug, kaug, vaug, *, seq):
    n_full = seq // LANES
    tail = seq - n_full * LANES
    n_blk = (seq - N_META) // ATT_BLK

    r = lax.broadcasted_iota(jnp.int32, (LANES, LANES), 0)
    c = lax.broadcasted_iota(jnp.int32, (LANES, LANES), 1)
    tri = jnp.where(c <= r, 1.0, 0.0).astype(BF16)
    carry = jnp.zeros((1, LANES), F32)
    for b in range(n_full + (1 if tail else 0)):
        nb = LANES if b < n_full else tail
        hi, mid, lo = _split3(lf_ref[b * LANES:b * LANES + nb, :])
        tb = tri[:nb, :nb]
        cb = _dot(tb, hi) + _dot(tb, mid) + _dot(tb, lo) + carry
        c_scr[b * LANES:b * LANES + nb, :] = cb
        carry = cb[nb - 1:nb, :]

    lane = lax.broadcasted_iota(jnp.int32, (seq, LANES), 1)
    cc = c_scr[...]
    hi = cc.astype(BF16).astype(F32)
    r1 = cc - hi
    mid = r1.astype(BF16).astype(F32)
    lo = r1 - mid
    packed = jnp.where(lane < H_A, hi,
                       jnp.where(lane < 2 * H_A, pltpu.roll(mid, H_A, 1),
                                 jnp.where(lane < PACK_ONE, pltpu.roll(lo, 2 * H_A, 1),
                                           jnp.where(lane == PACK_ONE, 1.0, 0.0)))).astype(BF16)
    for p in range(H_A // 2):
        aug = _dot(packed, sel_ref[p])
        qp = q_ref[:, p * LANES:(p + 1) * LANES].astype(F32)
        kp = k_ref[:, p * LANES:(p + 1) * LANES]
        vp = v_ref[:, p * LANES:(p + 1) * LANES]
        for hh in range(2):
            h, x0 = 2 * p + hh, (1 - hh) * DH_A
            own = (lane >= hh * DH_A) & (lane < (hh + 1) * DH_A)
            qaug[h] = jnp.where(own, qp, aug[:, hh * LANES:(hh + 1) * LANES]).astype(BF16)
            kaug[h] = jnp.where(own, kp, aug[:, (2 + hh) * LANES:(3 + hh) * LANES]).astype(BF16)
            vaug[h] = jnp.where(own, vp, jnp.where(lane == x0, 1.0, 0.0)).astype(BF16)

    def first(h, qa, nq):
        s = _dot_nt(qa, kaug[h, 0:N_META, :])
        if nq == N_META:
            rr = lax.broadcasted_iota(jnp.int32, s.shape, 0)
            cc2 = lax.broadcasted_iota(jnp.int32, s.shape, 1)
            s = jnp.where(rr >= cc2, s, NEG)
        m = jnp.max(s, axis=1, keepdims=True)
        acc = _dot(jnp.exp(s - m).astype(BF16), vaug[h, 0:N_META, :])
        return m, acc

    def steps(q0, start, carry, causal):
        qs = [qaug[h, pl.ds(q0, ATT_BLK), :] for h in range(H_A)]
        ks = [kaug[h, pl.ds(start, ATT_BLK), :] for h in range(H_A)]
        vs = [vaug[h, pl.ds(start, ATT_BLK), :] for h in range(H_A)]
        new = []
        for h in range(H_A):
            m_old, acc = carry[h]
            s = _dot_nt(qs[h], ks[h])
            if causal:
                rr = lax.broadcasted_iota(jnp.int32, s.shape, 0)
                cc2 = lax.broadcasted_iota(jnp.int32, s.shape, 1)
                s = jnp.where(rr >= cc2, s, NEG)
            m_new = jnp.maximum(m_old, jnp.max(s, axis=1, keepdims=True))
            pr = jnp.exp(s - m_new).astype(BF16)
            new.append((m_new, jnp.exp(m_old - m_new) * acc + _dot(pr, vs[h])))
        return tuple(new)

    def normalised(acc0, acc1, nq):
        ln = lax.broadcasted_iota(jnp.int32, (nq, LANES), 1)
        return jnp.where(ln < DH_A, acc0 / acc0[:, DH_A:DH_A + 1], acc1 / acc1[:, 0:1])

    for p in range(H_A // 2):
        a0 = first(2 * p, qaug[2 * p, 0:N_META, :], N_META)[1]
        a1 = first(2 * p + 1, qaug[2 * p + 1, 0:N_META, :], N_META)[1]
        o_ref[0:N_META, p * LANES:(p + 1) * LANES] = normalised(a0, a1, N_META).astype(o_ref.dtype)

    def q_block(i, _):
        q0 = pl.multiple_of(N_META + i * ATT_BLK, N_META)
        carry = tuple(first(h, qaug[h, pl.ds(q0, ATT_BLK), :], ATT_BLK) for h in range(H_A))

        def kv_block(j, carry):
            return steps(q0, pl.multiple_of(N_META + j * ATT_BLK, N_META), carry, False)

        carry = lax.fori_loop(0, i, kv_block, carry)
        carry = steps(q0, q0, carry, True)
        for p in range(H_A // 2):
            o_ref[pl.ds(q0, ATT_BLK), p * LANES:(p + 1) * LANES] = normalised(
                carry[2 * p][1], carry[2 * p + 1][1], ATT_BLK).astype(o_ref.dtype)
        return 0

    lax.fori_loop(0, n_blk, q_block, 0)


def _fox_prompt(qa, ka, va, l, lf, sel):
    b, seq, _ = qa.shape
    assert (seq - N_META) % ATT_BLK == 0
    full = lambda n: pl.BlockSpec((None, seq, n), lambda i: (i, 0, 0))
    layer = pl.BlockSpec((None, None, seq, D_A), lambda i: (l, i, 0, 0))
    heads = lambda: pltpu.VMEM((H_A, seq, LANES), BF16)
    return pl.pallas_call(
        functools.partial(_fox_prompt_kernel, seq=seq),
        grid=(b,),
        in_specs=[full(D_A), layer, layer, full(LANES),
                  _resident(sel.shape, lambda i: (0, 0, 0))],
        out_specs=full(D_A),
        out_shape=jax.ShapeDtypeStruct((b, seq, D_A), BF16),
        scratch_shapes=[pltpu.VMEM((seq, LANES), F32), heads(), heads(), heads()],
        compiler_params=_cparams(1),
        name="fox_prompt",
    )(qa, ka, va, lf, sel)


def _lane_cumsum_kernel(lf_ref, t_ref, c_ref):
    hi, mid, lo = _split3(lf_ref[...])
    t = t_ref[...]
    c_ref[...] = _dot(hi, t) + _dot(mid, t) + _dot(lo, t)


def _lane_cumsum(x, tri_u):
    rows = x.shape[0]
    tm = _row_tile(rows, 4096)
    spec = pl.BlockSpec((tm, LANES), lambda i: (i, 0))
    return pl.pallas_call(
        _lane_cumsum_kernel,
        grid=(rows // tm,),
        in_specs=[spec, _resident((LANES, LANES), lambda i: (0, 0))],
        out_specs=spec,
        out_shape=jax.ShapeDtypeStruct((rows, LANES), F32),
        compiler_params=_cparams(1),
        name="lane_cumsum",
    )(x, tri_u)


def _fox_sample_kernel(pt_ref, q_ref, *refs, n_steps, g_pages, n_tok):
    del pt_ref
    k_refs, v_refs, c_refs = refs[:g_pages], refs[g_pages:2 * g_pages], refs[2 * g_pages:3 * g_pages]
    kn_ref, vn_ref, lfn_ref, t_ref, o_ref, m_scr, l_scr, acc_scr, carry_scr = refs[3 * g_pages:]
    p = pl.program_id(1)
    page = k_refs[0].shape[-1]
    nq = q_ref.shape[0]
    q = q_ref[...]

    def update(s, pv):
        m_old = m_scr[...]
        m_new = jnp.maximum(m_old, jnp.max(s, axis=1, keepdims=True))
        alpha = jnp.exp(m_old - m_new)
        pr = jnp.exp(s - m_new)
        m_scr[...] = m_new
        l_scr[...] = alpha * l_scr[...] + jnp.sum(pr, axis=1, keepdims=True)
        acc_scr[...] = alpha * acc_scr[...] + pv(pr.astype(BF16))

    @pl.when(p == 0)
    def _():
        m_scr[...] = jnp.full(m_scr.shape, NEG, F32)
        l_scr[...] = jnp.zeros(l_scr.shape, F32)
        acc_scr[...] = jnp.zeros(acc_scr.shape, F32)
        carry_scr[...] = jnp.zeros(carry_scr.shape, F32)
        hi, mid, lo = _split3(jnp.broadcast_to(lfn_ref[...], (SUBLANES, LANES)))
        t = t_ref[...]
        c_new = (_dot(hi, t) + _dot(mid, t) + _dot(lo, t))[0:1, :nq]
        sn = _dot_nt(q, kn_ref[...]) - c_new
        rr = lax.broadcasted_iota(jnp.int32, sn.shape, 0)
        cc = lax.broadcasted_iota(jnp.int32, sn.shape, 1)
        ok = (rr // n_tok == cc // n_tok) & (cc % n_tok <= rr % n_tok)
        update(jnp.where(ok, sn, NEG), lambda pb: _dot(pb, vn_ref[...]))

    def per_head_rows(x):
        return jnp.concatenate([jnp.broadcast_to(x[h:h + 1, :], (n_tok, x.shape[1])) for h in range(H_A)], axis=0)

    carry = carry_scr[...]
    bias, ks, vs = [], [], []
    for g in range(g_pages):
        cin = c_refs[g][...]
        tot = cin[:, page - 1:page]
        bias.append(per_head_rows((tot - cin) + carry))
        carry = carry + tot
        ks.append(k_refs[g][...].reshape(D_A, page).astype(BF16))
        vs.append(v_refs[g][...].reshape(D_A, page).astype(BF16))
    carry_scr[...] = carry
    v_all = jnp.concatenate(vs, axis=1)
    s = _dot(q, jnp.concatenate(ks, axis=1)) + jnp.concatenate(bias, axis=1)
    update(s, lambda pb: _dot_nt(pb, v_all))

    @pl.when(p == n_steps - 1)
    def _():
        acc = acc_scr[...] / l_scr[...]
        o_ref[...] = jnp.concatenate(
            [acc[h * n_tok:(h + 1) * n_tok, h * DH_A:(h + 1) * DH_A] for h in range(H_A)], axis=0).astype(o_ref.dtype)


def _fox_sample(l, page_flat, q_hq, cache_kt, cache_vt, cpool, k_new, v_new, lf_new, tri_u, n_pages):
    n_seq, nq, _ = q_hq.shape
    page = cache_kt.shape[-1]
    g_pages = max(g for g in range(1, PAGES_PER_STEP + 1) if n_pages % g == 0)
    n_steps = n_pages // g_pages
    slot = lambda s, p, g: s * n_pages + (n_pages - 1 - (p * g_pages + g))
    idx = lambda g: (lambda s, p, pt: (l, pt[slot(s, p, g)], 0, 0, 0))
    cidx = lambda g: (lambda s, p, pt: (l, pt[slot(s, p, g)], 0, 0))
    kv_specs = [pl.BlockSpec((None, None, H_A, DH_A, page), idx(g)) for g in range(g_pages)]
    c_specs = [pl.BlockSpec((None, None, H_A, page), cidx(g)) for g in range(g_pages)]
    seq_spec = lambda a, b: pl.BlockSpec((None, a, b), lambda s, p, pt: (s, 0, 0))
    grid_spec = pltpu.PrefetchScalarGridSpec(
        num_scalar_prefetch=1,
        grid=(n_seq, n_steps),
        in_specs=[seq_spec(nq, D_A)] + kv_specs + kv_specs + c_specs
                 + [seq_spec(nq, D_A), seq_spec(nq, D_A), seq_spec(1, LANES),
                    pl.BlockSpec((LANES, LANES), lambda s, p, pt: (0, 0))],
        out_specs=seq_spec(nq, DH_A),
        scratch_shapes=[pltpu.VMEM((nq, 1), F32), pltpu.VMEM((nq, 1), F32),
                        pltpu.VMEM((nq, D_A), F32), pltpu.VMEM((H_A, 1), F32)],
    )
    return pl.pallas_call(
        functools.partial(_fox_sample_kernel, n_steps=n_steps, g_pages=g_pages, n_tok=nq // H_A),
        grid_spec=grid_spec,
        out_shape=jax.ShapeDtypeStruct((n_seq, nq, DH_A), BF16),
        compiler_params=_cparams(2),
        name="fox_sample",
    )(page_flat, q_hq, *([cache_kt] * g_pages), *([cache_vt] * g_pages), *([cpool] * g_pages),
      k_new, v_new, lf_new, tri_u)


CONV_PAD = 32


def _conv_kernel(u_ref, buf_ref, w_ref, b_ref, g_ref, beta_ref, act_ref, nbuf_ref, ext, dw, *, nb, seq, tr):
    n_tiles = -(-seq // tr)
    keep = CONV_W - 1
    for s in range(nb):
        ext[s, 0:CONV_PAD - keep, :] = jnp.zeros((CONV_PAD - keep, C_B), F32)
        ext[s, CONV_PAD - keep:CONV_PAD, :] = buf_ref[s]
        ext[s, CONV_PAD:CONV_PAD + seq, :] = u_ref[s]
        if n_tiles * tr > seq:
            ext[s, CONV_PAD + seq:CONV_PAD + n_tiles * tr, :] = jnp.zeros((n_tiles * tr - seq, C_B), F32)
        nbuf_ref[s] = ext[s, CONV_PAD + seq - keep:CONV_PAD + seq, :]

        def tile(i, _, s=s):
            r0 = pl.multiple_of(i * tr, SUBLANES)
            for c in range(C_B // LANES):
                win = ext[s, pl.ds(r0, tr + CONV_PAD), c * LANES:(c + 1) * LANES]
                shifted = [win] + [pltpu.roll(win, tr + CONV_PAD - sh, 0) for sh in range(1, SUBLANES)]
                acc = jnp.broadcast_to(b_ref[:, c * LANES:(c + 1) * LANES], (tr, LANES))
                for j in range(CONV_W):
                    a, sh = divmod(CONV_PAD - keep + j, SUBLANES)
                    acc = acc + shifted[sh][a * SUBLANES:a * SUBLANES + tr, :] * w_ref[j:j + 1, c * LANES:(c + 1) * LANES]
                dw[pl.ds(r0, tr), c * LANES:(c + 1) * LANES] = acc
            return 0

        lax.fori_loop(0, n_tiles, tile, 0)

        def norm(r0, n, s=s):
            x = dw[pl.ds(r0, n), :]
            mu = jnp.mean(x, axis=-1, keepdims=True)
            xc = x - mu
            var = jnp.mean(xc * xc, axis=-1, keepdims=True)
            y = xc * lax.rsqrt(var + EPS) * g_ref[...] + beta_ref[...]
            act_ref[s, pl.ds(r0, n), :] = (y * _sigmoid(y)).astype(act_ref.dtype)

        n_full = seq // tr

        def norm_tile(i, _):
            norm(pl.multiple_of(i * tr, SUBLANES), tr)
            return 0

        lax.fori_loop(0, n_full, norm_tile, 0)
        if seq > n_full * tr:
            norm(n_full * tr, seq - n_full * tr)


def _conv(u, buf, l, w_dw, b_dw, ln_g, ln_b, nb):
    bsz, seq, _ = u.shape
    tr = LANES if seq >= LANES else seq
    n_tiles = -(-seq // tr)
    keep = CONV_W - 1
    par = lambda r: pl.BlockSpec((None, r, C_B), lambda i: (l, 0, 0))
    if buf.ndim == 4:
        buf_spec = pl.BlockSpec((None, nb, keep, C_B), lambda i: (l, i, 0, 0))
    else:
        buf_spec = pl.BlockSpec((nb, keep, C_B), lambda i: (i, 0, 0))
    return pl.pallas_call(
        functools.partial(_conv_kernel, nb=nb, seq=seq, tr=tr),
        grid=(bsz // nb,),
        in_specs=[pl.BlockSpec((nb, seq, C_B), lambda i: (i, 0, 0)), buf_spec,
                  par(CONV_PAD), par(1), par(1), par(1)],
        out_specs=[pl.BlockSpec((nb, seq, C_B), lambda i: (i, 0, 0)),
                   pl.BlockSpec((nb, keep, C_B), lambda i: (i, 0, 0))],
        out_shape=[jax.ShapeDtypeStruct((bsz, seq, C_B), BF16),
                   jax.ShapeDtypeStruct((bsz, keep, C_B), F32)],
        scratch_shapes=[pltpu.VMEM((nb, CONV_PAD + n_tiles * tr, C_B), F32),
                        pltpu.VMEM((n_tiles * tr, C_B), F32)],
        compiler_params=_cparams(1),
        name="conv",
    )(u, buf, w_dw, b_dw, ln_g, ln_b)


def _ret_chunk(s, q, k, v, n, lg):
    ii = lax.broadcasted_iota(jnp.int32, (n, n), 0)
    jj = lax.broadcasted_iota(jnp.int32, (n, n), 1)
    rel = (ii - jj).astype(F32)
    decay = jnp.where(rel >= 0, jnp.exp(jnp.maximum(rel, 0.0) * lg), 0.0)
    scores = _dot_nt(q, k.astype(BF16)) * decay
    o = _dot(scores.astype(BF16), v)
    jc = lax.broadcasted_iota(jnp.int32, (n, 1), 0).astype(F32)
    o = o + _dot(q, s.astype(BF16)) * jnp.exp((jc + 1.0) * lg)
    ks = (k * jnp.exp((n - 1.0 - jc) * lg)).astype(BF16)
    kv = lax.dot_general(ks, v, (((0,), (0,)), ((), ())), preferred_element_type=F32)
    gamma_n = float(np.exp(n * lg)) if isinstance(lg, float) else jnp.exp(n * lg)
    return gamma_n * s + kv, o


def _group_norm(o, g):
    mu = jnp.mean(o, axis=-1, keepdims=True)
    oc = o - mu
    var = jnp.mean(oc * oc, axis=-1, keepdims=True)
    return oc * lax.rsqrt(var + EPS) * g


def _head_log_gamma(h):
    hv = jnp.full((1, 1), h, jnp.int32)
    lg = jnp.full((1, 1), _LOG_GAMMA[H_C - 1], F32)
    for i in range(H_C - 2, -1, -1):
        lg = jnp.where(hv == i, jnp.float32(_LOG_GAMMA[i]), lg)
    return lg


def _ret_prompt_kernel(q_ref, k_ref, v_ref, g_ref, *rest, seq):
    o_ref, s_ref = rest[-2:]

    def chunk(r0, n, states):
        qs = [q_ref[pl.ds(r0, n), h * DK_C:(h + 1) * DK_C] for h in range(H_C)]
        ks = [k_ref[pl.ds(r0, n), h * DK_C:(h + 1) * DK_C] for h in range(H_C)]
        vs = [v_ref[pl.ds(r0, n), h * DV_C:(h + 1) * DV_C] for h in range(H_C)]
        new, outs = [], []
        for h in range(H_C):
            s, o = _ret_chunk(states[h], qs[h], ks[h], vs[h], n, _LOG_GAMMA[h])
            new.append(s)
            outs.append(_group_norm(o, g_ref[:, h * DV_C:(h + 1) * DV_C]))
        o_ref[pl.ds(r0, n), :] = jnp.concatenate(outs, axis=1).astype(o_ref.dtype)
        return tuple(new)

    states = chunk(0, N_META, tuple(jnp.zeros((DK_C, DV_C), F32) for _ in range(H_C)))
    states = lax.fori_loop(0, (seq - N_META) // RET_BLK,
                           lambda i, st: chunk(pl.multiple_of(N_META + i * RET_BLK, N_META), RET_BLK, st), states)
    for h in range(H_C):
        s_ref[h] = states[h]


def _ret_prompt(qr, kr, vc, l, gn_g, s_prev):
    b, seq, _ = qr.shape
    depth = gn_g.shape[0]
    assert (seq - N_META) % RET_BLK == 0
    full = lambda n: pl.BlockSpec((None, seq, n), lambda i: (i, 0, 0))
    s_spec, s_shape, extra = _stacked_out((depth, b, H_C, DK_C, DV_C), (None, H_C, DK_C, DV_C),
                                          lambda i: (i, 0, 0, 0), l, s_prev)
    return pl.pallas_call(
        functools.partial(_ret_prompt_kernel, seq=seq),
        grid=(b,),
        in_specs=[full(D_QK_C), full(D_QK_C), full(D_V_C), pl.BlockSpec((None, 1, D_V_C), lambda i: (l, 0, 0))]
                 + [s for s, _ in extra],
        out_specs=[full(D_V_C), s_spec],
        out_shape=[jax.ShapeDtypeStruct((b, seq, D_V_C), BF16), s_shape],
        input_output_aliases={4: 1} if extra else {},
        compiler_params=_cparams(1),
        name="ret_prompt",
    )(qr, kr, vc, gn_g, *[a for _, a in extra])


def _ret_sample_kernel(q_ref, k_ref, v_ref, g_ref, s0_ref, *rest, nb, n_tok):
    o_ref, s_ref = rest[-2:]
    lg = _head_log_gamma(pl.program_id(1))
    g = g_ref[...]
    for s in range(nb):
        s_new, o = _ret_chunk(s0_ref[s], q_ref[s], k_ref[s], v_ref[s], n_tok, lg)
        o_ref[s] = _group_norm(o, g).astype(o_ref.dtype)
        s_ref[s] = s_new


def _ret_sample(qr, kr, vc, l, gn_g, state, nb, s_prev):
    n_seq, n_tok, _ = qr.shape
    qk = pl.BlockSpec((nb, n_tok, DK_C), lambda i, h: (i, 0, h))
    vv = pl.BlockSpec((nb, n_tok, DV_C), lambda i, h: (i, 0, h))
    s_spec, s_shape, extra = _stacked_out(state.shape, (nb, None, DK_C, DV_C), lambda i, h: (i, h, 0, 0), l, s_prev)
    return pl.pallas_call(
        functools.partial(_ret_sample_kernel, nb=nb, n_tok=n_tok),
        grid=(n_seq // nb, H_C),
        in_specs=[qk, qk, vv, pl.BlockSpec((None, 1, DV_C), lambda i, h: (l, 0, h)),
                  pl.BlockSpec((None, nb, None, DK_C, DV_C), lambda i, h: (l, i, h, 0, 0))] + [s for s, _ in extra],
        out_specs=[vv, s_spec],
        out_shape=[jax.ShapeDtypeStruct((n_seq, n_tok, D_V_C), BF16), s_shape],
        input_output_aliases={5: 1} if extra else {},
        compiler_params=_cparams(2),
        name="ret_sample",
    )(qr, kr, vc, gn_g, state, *[a for _, a in extra])


def _merge_kernel(x_ref, oa_ref, cb_ref, on_ref, gpre_ref, gpost_ref, wgc_ref, wgate_ref, wao_ref,
                  wpw_ref, wro_ref, wo_ref, y_ref):
    x = x_ref[...]
    h = _rms(x, gpre_ref[...]).astype(BF16)
    zc = _dot(h, wgc_ref[...])
    y_c = _dot((zc * _sigmoid(zc) * on_ref[...]).astype(BF16), wro_ref[...])
    y_a = _dot(oa_ref[...], wao_ref[...])
    y_b = _dot(cb_ref[...], wpw_ref[...])
    merged = (_sigmoid(_dot(h, wgate_ref[:, 0:D_MODEL])) * y_a
              + _sigmoid(_dot(h, wgate_ref[:, D_MODEL:2 * D_MODEL])) * y_b
              + _sigmoid(_dot(h, wgate_ref[:, 2 * D_MODEL:])) * y_c)
    out = _dot(merged.astype(BF16), wo_ref[...])
    y_ref[...] = x + _rms(out, gpost_ref[...])


def _merge(x, oa, cb, on, l, gpre, gpost, wgc, wgate, wao, wpw, wro, wo, tm):
    t = x.shape[0]
    row = lambda n: pl.BlockSpec((tm, n), lambda i: (i, 0))
    w = lambda k, n: _resident((None, k, n), lambda i: (l, 0, 0))
    return pl.pallas_call(
        _merge_kernel,
        grid=(t // tm,),
        in_specs=[row(D_MODEL), row(D_A), row(C_B), row(D_V_C), w(1, D_MODEL), w(1, D_MODEL),
                  w(D_MODEL, D_V_C), w(D_MODEL, N_BRANCH * D_MODEL), w(D_A, D_MODEL), w(C_B, D_MODEL),
                  w(D_V_C, D_MODEL), w(D_MODEL, D_MODEL)],
        out_specs=row(D_MODEL),
        out_shape=jax.ShapeDtypeStruct((t, D_MODEL), F32),
        compiler_params=_cparams(1),
        name="merge",
    )(x, oa, cb, on, gpre, gpost, wgc, wgate, wao, wpw, wro, wo)


FF_CHUNK = 1024


def _ffn_kernel(x_ref, gpre_ref, gpost_ref, wup_ref, wdn_ref, y_ref):
    x = x_ref[...]
    h = _rms(x, gpre_ref[...]).astype(BF16)
    ff = jnp.zeros(x.shape, F32)
    for c in range(D_FF // FF_CHUNK):
        up = jnp.maximum(_dot(h, wup_ref[:, c * FF_CHUNK:(c + 1) * FF_CHUNK]), 0.0)
        ff = ff + _dot((up * up).astype(BF16), wdn_ref[c * FF_CHUNK:(c + 1) * FF_CHUNK, :])
    y_ref[...] = x + _rms(ff, gpost_ref[...])


def _ffn(x, l, gpre, gpost, wup, wdn, tm):
    t = x.shape[0]
    row = pl.BlockSpec((tm, D_MODEL), lambda i: (i, 0))
    w = lambda k, n: _resident((None, k, n), lambda i: (l, 0, 0))
    return pl.pallas_call(
        _ffn_kernel,
        grid=(t // tm,),
        in_specs=[row, w(1, D_MODEL), w(1, D_MODEL), w(D_MODEL, D_FF), w(D_FF, D_MODEL)],
        out_specs=row,
        out_shape=jax.ShapeDtypeStruct((t, D_MODEL), F32),
        compiler_params=_cparams(1),
        name="ffn",
    )(x, gpre, gpost, wup, wdn)


def _rope_tables(pos):
    half = DK_C // 2
    inv = ROPE_BASE ** (-jnp.arange(half, dtype=F32) / half)
    ang = pos.astype(F32)[:, None] * inv[None, :]
    cos, sin = jnp.cos(ang), jnp.sin(ang)
    return jnp.concatenate([cos, cos], axis=1), jnp.concatenate([-sin, sin], axis=1)


def kernel(x_prompt, x_sample, cache_k, cache_v, cache_logf, state_conv, state_ret, page_table,
           meta_tokens, norm_mix_pre, norm_mix_post, norm_ffn_pre, norm_ffn_post,
           w_in, b_forget, w_dw, b_dw, ln_conv_g, ln_conv_b, w_pw_out, w_attn_out,
           gn_ret_g, w_ret_out, w_o, w_ff_up, w_ff_down):
    depth = w_in.shape[0]
    bp, seq_p, _ = x_prompt.shape
    n_dec, n_tok, _ = x_sample.shape
    n_pages = page_table.shape[1]
    n_pool, page = cache_k.shape[1], cache_k.shape[2]
    assert page == LANES
    past_len = n_pages * page
    lp = N_META + seq_p
    tp, ts = bp * lp, n_dec * n_tok

    cuts = np.cumsum([0, D_A, D_A, D_A, H_A, 2 * C_B, D_QK_C, D_QK_C, D_V_C, D_V_C, N_BRANCH * D_MODEL])
    seg = lambda a, b: w_in[:, :, int(cuts[a]):int(cuts[b])].astype(BF16)
    wa, wglu, wqk, wv, wgc, wgate = seg(0, 3), seg(4, 5), seg(5, 7), seg(7, 8), seg(8, 9), seg(9, 10)
    wf = jnp.pad(w_in[:, :, int(cuts[3]):int(cuts[4])], ((0, 0), (0, 0), (0, LANES - H_A))).astype(BF16)
    bfp = jnp.pad(b_forget, ((0, 0), (0, LANES - H_A)))[:, None, :]
    wao, wpw, wro, wo = (w.astype(BF16) for w in (w_attn_out, w_pw_out, w_ret_out, w_o))
    wup, wdn = w_ff_up.astype(BF16), w_ff_down.astype(BF16)
    vec = lambda a: a[:, None, :]
    g_mpre, g_mpost, g_fpre, g_fpost = vec(norm_mix_pre), vec(norm_mix_post), vec(norm_ffn_pre), vec(norm_ffn_post)
    wdw = jnp.pad(w_dw, ((0, 0), (0, CONV_PAD - CONV_W), (0, 0)))
    bdw, lng, lnb, gng = vec(b_dw), vec(ln_conv_g), vec(ln_conv_b), vec(gn_ret_g)

    tm_in_p = _row_tile(tp, 400, must_divide=lp)
    tm_in_s = _row_tile(ts, 256)
    tm_p = _row_tile(tp, 400)
    tm_s = _row_tile(ts, 512)
    cos_p, sin_p = _rope_tables(jnp.arange(lp))
    cos_s, sin_s = _rope_tables(past_len + jnp.arange(n_tok))
    cos_s, sin_s = (jnp.tile(a, (tm_in_s // n_tok, 1)) for a in (cos_s, sin_s))
    sel = _fox_aug_select()

    tri_u = jnp.asarray(np.triu(np.ones((LANES, LANES), np.float32)), dtype=BF16)
    cache_kt = cache_k.transpose(0, 1, 3, 4, 2)
    cache_vt = cache_v.transpose(0, 1, 3, 4, 2)
    lf_t = cache_logf.transpose(0, 1, 3, 2).reshape(depth * n_pool * H_A, page)
    cpool = _lane_cumsum(lf_t, tri_u).reshape(depth, n_pool, H_A, page)
    page_flat = page_table.reshape(-1)

    xp = jnp.concatenate([jnp.broadcast_to(meta_tokens[None], (bp, N_META, D_MODEL)), x_prompt], axis=1)
    xp = xp.reshape(tp, D_MODEL)
    xs = x_sample.reshape(ts, D_MODEL)
    buf0 = jnp.zeros((bp, CONV_W - 1, C_B), F32)
    nq = H_A * n_tok
    assert nq <= LANES
    own_head = jnp.asarray(np.arange(nq)[:, None] // n_tok == np.arange(D_A)[None, :] // DH_A)
    lane_grp = np.arange(LANES)
    t_grp = jnp.asarray(((lane_grp[:, None] // n_tok == lane_grp[None, :] // n_tok)
                         & (lane_grp[:, None] <= lane_grp[None, :])).astype(np.float32), dtype=BF16)

    def head_major(a):
        a = jnp.broadcast_to(a.reshape(n_dec, 1, n_tok, D_A), (n_dec, H_A, n_tok, D_A)).reshape(n_dec, nq, D_A)
        return jnp.where(own_head, a, 0).astype(BF16)

    p_lf, p_conv, s_lf, s_conv = [], [], [], []
    pk = pv = p_ret = sk = sv = s_ret = None
    for l in range(depth):
        qa, pk, pv, lf, u, qr, kr, vc = _inproj(xp, l, g_mpre, wa, wf, bfp, wglu, wqk, wv, cos_p, sin_p,
                                                tm_in_p, pk, pv)
        r3 = lambda a: a.reshape(bp, lp, a.shape[-1])
        oa = _fox_prompt(r3(qa), pk.reshape(depth, bp, lp, D_A), pv.reshape(depth, bp, lp, D_A), l, r3(lf), sel)
        cact, nbuf = _conv(r3(u), buf0, l, wdw, bdw, lng, lnb, 1)
        on, p_ret = _ret_prompt(r3(qr), r3(kr), r3(vc), l, gng, p_ret)
        xp = _merge(xp, oa.reshape(tp, D_A), cact.reshape(tp, C_B), on.reshape(tp, D_V_C), l,
                    g_mpre, g_mpost, wgc, wgate, wao, wpw, wro, wo, tm_p)
        xp = _ffn(xp, l, g_fpre, g_fpost, wup, wdn, tm_p)
        p_lf.append(lf[:, :H_A].reshape(bp, lp, H_A))
        p_conv.append(nbuf)

        qa, sk, sv, lf, u, qr, kr, vc = _inproj(xs, l, g_mpre, wa, wf, bfp, wglu, wqk, wv, cos_s, sin_s,
                                                tm_in_s, sk, sv)
        s3 = lambda a: a.reshape(n_dec, n_tok, a.shape[-1])
        lf8 = lf[:, :H_A]
        lf_new = jnp.pad(lf8.reshape(n_dec, n_tok, H_A).transpose(0, 2, 1).reshape(n_dec, 1, nq),
                         ((0, 0), (0, 0), (0, LANES - nq)))
        o_hq = _fox_sample(l, page_flat, head_major(qa), cache_kt, cache_vt, cpool,
                           head_major(sk[l]), head_major(sv[l]), lf_new, t_grp, n_pages)
        oa = o_hq.reshape(n_dec, H_A, n_tok, DH_A).transpose(0, 2, 1, 3).reshape(ts, D_A)
        cact, nbuf = _conv(s3(u), state_conv, l, wdw, bdw, lng, lnb, 16)
        on, s_ret = _ret_sample(s3(qr), s3(kr), s3(vc), l, gng, state_ret, 8, s_ret)
        xs = _merge(xs, oa, cact.reshape(ts, C_B), on.reshape(ts, D_V_C), l,
                    g_mpre, g_mpost, wgc, wgate, wao, wpw, wro, wo, tm_s)
        xs = _ffn(xs, l, g_fpre, g_fpost, wup, wdn, tm_s)
        s_lf.append(lf8.reshape(n_dec, n_tok, H_A))
        s_conv.append(nbuf)

    y_prompt = xp.reshape(bp, lp, D_MODEL)[:, N_META:]
    y_sample = xs.reshape(n_dec, n_tok, D_MODEL)
    return (y_prompt, y_sample,
            pk.reshape(depth, bp, lp, H_A, DH_A), pv.reshape(depth, bp, lp, H_A, DH_A),
            jnp.stack(p_lf), jnp.stack(p_conv), p_ret,
            sk.reshape(depth, n_dec, n_tok, H_A, DH_A), sv.reshape(depth, n_dec, n_tok, H_A, DH_A),
            jnp.stack(s_lf), jnp.stack(s_conv), s_ret)
```

```python
import functools

import numpy as np
import jax
import jax.numpy as jnp
from jax import lax
from jax.experimental import pallas as pl
from jax.experimental.pallas import tpu as pltpu

D_MODEL = 1024
N_META = 16
H_A = 8
DH_A = 64
D_A = H_A * DH_A
C_B = 512
CONV_W = 31
H_C = 4
DK_C = 128
DV_C = 256
D_QK_C = H_C * DK_C
D_V_C = H_C * DV_C
N_BRANCH = 3
D_FF = 4 * D_MODEL
EPS = 1e-6
ROPE_BASE = 10000.0

LANES = 128
SUBLANES = 8
ATT_BLK = 512
RET_BLK = 128
NEG = -1e30
VMEM_LIMIT = 56 * 1024 * 1024
PAGES_PER_STEP = 16

BF16 = jnp.bfloat16
F32 = jnp.float32

_LOG_GAMMA = tuple(float(np.log1p(-(2.0 ** (-5.0 - h)))) for h in range(H_C))


def _cparams(n_axes):
    return pltpu.CompilerParams(dimension_semantics=("arbitrary",) * n_axes,
                                vmem_limit_bytes=VMEM_LIMIT)


def _row_tile(total, target, must_divide=None):
    best = None
    for t in range(SUBLANES, min(total, target) + 1, SUBLANES):
        if total % t == 0 and (must_divide is None or must_divide % t == 0):
            best = t
    assert best is not None, (total, target, must_divide)
    return best


def _resident(shape, index_map):
    return pl.BlockSpec(shape, index_map, pipeline_mode=pl.Buffered(1))


def _dot(a, b):
    return jnp.dot(a, b, preferred_element_type=F32)


def _dot_nt(a, b):
    return lax.dot_general(a, b, (((1,), (1,)), ((), ())), preferred_element_type=F32)


def _split3(x):
    hi = x.astype(BF16)
    r1 = x - hi.astype(F32)
    mid = r1.astype(BF16)
    r2 = r1 - mid.astype(F32)
    return hi, mid, r2.astype(BF16)


def _rms(x, g):
    return x * lax.rsqrt(jnp.mean(x * x, axis=-1, keepdims=True) + EPS) * g


def _log_sigmoid(z):
    return jnp.minimum(z, 0.0) - jnp.log1p(jnp.exp(-jnp.abs(z)))


def _sigmoid(z):
    return 1.0 / (1.0 + jnp.exp(-z))


def _stacked_out(shape, block, index_map, l, prev):
    spec = pl.BlockSpec((None,) + block, lambda *i: (l,) + index_map(*i))
    extra_in = [] if prev is None else [(pl.BlockSpec(memory_space=pl.ANY), prev)]
    return spec, jax.ShapeDtypeStruct(shape, F32), extra_in


def _inproj_kernel(x_ref, g_ref, wa_ref, wf_ref, bf_ref, wglu_ref, wqk_ref, wv_ref, cos_ref, sin_ref, *rest):
    qa_ref, ka_ref, va_ref, lf_ref, u_ref, qr_ref, kr_ref, vc_ref = rest[-8:]
    h = _rms(x_ref[...], g_ref[...]).astype(BF16)
    za = _dot(h, wa_ref[...])
    qa_ref[...] = (za[:, :D_A] * (DH_A ** -0.5)).astype(BF16)
    ka_ref[...] = za[:, D_A:2 * D_A]
    va_ref[...] = za[:, 2 * D_A:]
    lf_ref[...] = _log_sigmoid(_dot(h, wf_ref[...]) + bf_ref[...])
    zg = _dot(h, wglu_ref[...])
    u_ref[...] = zg[:, :C_B] * _sigmoid(zg[:, C_B:])
    zqk = _dot(h, wqk_ref[...])
    cos = cos_ref[...]
    sin = sin_ref[...]
    for j in range(2 * H_C):
        xh = zqk[:, j * DK_C:(j + 1) * DK_C]
        r = xh * cos + pltpu.roll(xh, DK_C // 2, 1) * sin
        if j < H_C:
            qr_ref[:, j * DK_C:(j + 1) * DK_C] = r.astype(BF16)
        else:
            kr_ref[:, (j - H_C) * DK_C:(j - H_C + 1) * DK_C] = r * (DK_C ** -0.5)
    vc_ref[...] = _dot(h, wv_ref[...]).astype(BF16)


def _inproj(x, l, gpre, wa, wf, bfp, wglu, wqk, wv, cos_t, sin_t, tm, k_prev, v_prev):
    t = x.shape[0]
    depth = wa.shape[0]
    n_tab = cos_t.shape[0] // tm
    row = lambda n: pl.BlockSpec((tm, n), lambda i: (i, 0))
    wspec = lambda n: _resident((None, D_MODEL, n), lambda i: (l, 0, 0))
    tab = pl.BlockSpec((tm, LANES), lambda i: (i % n_tab, 0))
    outs = [(D_A, BF16), None, None, (LANES, F32), (C_B, F32), (D_QK_C, BF16), (D_QK_C, F32), (D_V_C, BF16)]
    out_specs = [None if o is None else row(o[0]) for o in outs]
    out_shape = [None if o is None else jax.ShapeDtypeStruct((t, o[0]), o[1]) for o in outs]
    in_specs = [row(D_MODEL), _resident((None, 1, D_MODEL), lambda i: (l, 0, 0)),
                wspec(3 * D_A), wspec(LANES), _resident((None, 1, LANES), lambda i: (l, 0, 0)),
                wspec(2 * C_B), wspec(2 * D_QK_C), wspec(D_V_C), tab, tab]
    args = [x, gpre, wa, wf, bfp, wglu, wqk, wv, cos_t, sin_t]
    aliases = {}
    for o, prev in ((1, k_prev), (2, v_prev)):
        out_specs[o], out_shape[o], extra = _stacked_out((depth, t, D_A), (tm, D_A), lambda i: (i, 0), l, prev)
        for spec, arr in extra:
            aliases[len(args)] = o
            in_specs.append(spec)
            args.append(arr)
    return pl.pallas_call(
        _inproj_kernel,
        grid=(t // tm,),
        in_specs=in_specs,
        out_specs=out_specs,
        out_shape=out_shape,
        input_output_aliases=aliases,
        compiler_params=_cparams(1),
        name="inproj",
    )(*args)


N_AUG = 3
PACK_ONE = N_AUG * H_A


def _fox_aug_select():
    sel = np.zeros((H_A // 2, LANES, 4 * LANES), np.float32)
    for p in range(H_A // 2):
        for hh in range(2):
            h, x0 = 2 * p + hh, (1 - hh) * DH_A
            qb, kb = hh * LANES + x0, 2 * LANES + hh * LANES + x0
            for j in range(N_AUG):
                sel[p, j * H_A + h, qb + j] = 1.0
                sel[p, PACK_ONE, qb + N_AUG + j] = 1.0
                sel[p, PACK_ONE, kb + j] = 1.0
                sel[p, j * H_A + h, kb + N_AUG + j] = -1.0
    return jnp.asarray(sel, dtype=BF16)


def _fox_prompt_kernel(q_ref, k_ref, v_ref, lf_ref, sel_ref, o_ref, c_scr, qaug, kaug, vaug, *, seq):
    n_full = seq // LANES
    tail = seq - n_full * LANES
    n_blk = (seq - N_META) // ATT_BLK

    r = lax.broadcasted_iota(jnp.int32, (LANES, LANES), 0)
    c = lax.broadcasted_iota(jnp.int32, (LANES, LANES), 1)
    tri = jnp.where(c <= r, 1.0, 0.0).astype(BF16)
    carry = jnp.zeros((1, LANES), F32)
    for b in range(n_full + (1 if tail else 0)):
        nb = LANES if b < n_full else tail
        hi, mid, lo = _split3(lf_ref[b * LANES:b * LANES + nb, :])
        tb = tri[:nb, :nb]
        cb = _dot(tb, hi) + _dot(tb, mid) + _dot(tb, lo) + carry
        c_scr[b * LANES:b * LANES + nb, :] = cb
        carry = cb[nb - 1:nb, :]

    lane = lax.broadcasted_iota(jnp.int32, (seq, LANES), 1)
    cc = c_scr[...]
    hi = cc.astype(BF16).astype(F32)
    r1 = cc - hi
    mid = r1.astype(BF16).astype(F32)
    lo = r1 - mid
    packed = jnp.where(lane < H_A, hi,
                       jnp.where(lane < 2 * H_A, pltpu.roll(mid, H_A, 1),
                                 jnp.where(lane < PACK_ONE, pltpu.roll(lo, 2 * H_A, 1),
                                           jnp.where(lane == PACK_ONE, 1.0, 0.0)))).astype(BF16)
    for p in range(H_A // 2):
        aug = _dot(packed, sel_ref[p])
        qp = q_ref[:, p * LANES:(p + 1) * LANES].astype(F32)
        kp = k_ref[:, p * LANES:(p + 1) * LANES]
        vp = v_ref[:, p * LANES:(p + 1) * LANES]
        for hh in range(2):
            h, x0 = 2 * p + hh, (1 - hh) * DH_A
            own = (lane >= hh * DH_A) & (lane < (hh + 1) * DH_A)
            qaug[h] = jnp.where(own, qp, aug[:, hh * LANES:(hh + 1) * LANES]).astype(BF16)
            kaug[h] = jnp.where(own, kp, aug[:, (2 + hh) * LANES:(3 + hh) * LANES]).astype(BF16)
            vaug[h] = jnp.where(own, vp, jnp.where(lane == x0, 1.0, 0.0)).astype(BF16)

    def first(h, qa, nq):
        s = _dot_nt(qa, kaug[h, 0:N_META, :])
        if nq == N_META:
            rr = lax.broadcasted_iota(jnp.int32, s.shape, 0)
            cc2 = lax.broadcasted_iota(jnp.int32, s.shape, 1)
            s = jnp.where(rr >= cc2, s, NEG)
        m = jnp.max(s, axis=1, keepdims=True)
        acc = _dot(jnp.exp(s - m).astype(BF16), vaug[h, 0:N_META, :])
        return m, acc

    def steps(q0, start, carry, causal):
        qs = [qaug[h, pl.ds(q0, ATT_BLK), :] for h in range(H_A)]
        ks = [kaug[h, pl.ds(start, ATT_BLK), :] for h in range(H_A)]
        vs = [vaug[h, pl.ds(start, ATT_BLK), :] for h in range(H_A)]
        new = []
        for h in range(H_A):
            m_old, acc = carry[h]
            s = _dot_nt(qs[h], ks[h])
            if causal:
                rr = lax.broadcasted_iota(jnp.int32, s.shape, 0)
                cc2 = lax.broadcasted_iota(jnp.int32, s.shape, 1)
                s = jnp.where(rr >= cc2, s, NEG)
            m_new = jnp.maximum(m_old, jnp.max(s, axis=1, keepdims=True))
            pr = jnp.exp(s - m_new).astype(BF16)
            new.append((m_new, jnp.exp(m_old - m_new) * acc + _dot(pr, vs[h])))
        return tuple(new)

    def normalised(acc0, acc1, nq):
        ln = lax.broadcasted_iota(jnp.int32, (nq, LANES), 1)
        return jnp.where(ln < DH_A, acc0 / acc0[:, DH_A:DH_A + 1], acc1 / acc1[:, 0:1])

    for p in range(H_A // 2):
        a0 = first(2 * p, qaug[2 * p, 0:N_META, :], N_META)[1]
        a1 = first(2 * p + 1, qaug[2 * p + 1, 0:N_META, :], N_META)[1]
        o_ref[0:N_META, p * LANES:(p + 1) * LANES] = normalised(a0, a1, N_META).astype(o_ref.dtype)

    def q_block(i, _):
        q0 = pl.multiple_of(N_META + i * ATT_BLK, N_META)
        carry = tuple(first(h, qaug[h, pl.ds(q0, ATT_BLK), :], ATT_BLK) for h in range(H_A))

        def kv_block(j, carry):
            return steps(q0, pl.multiple_of(N_META + j * ATT_BLK, N_META), carry, False)

        carry = lax.fori_loop(0, i, kv_block, carry)
        carry = steps(q0, q0, carry, True)
        for p in range(H_A // 2):
            o_ref[pl.ds(q0, ATT_BLK), p * LANES:(p + 1) * LANES] = normalised(
                carry[2 * p][1], carry[2 * p + 1][1], ATT_BLK).astype(o_ref.dtype)
        return 0

    lax.fori_loop(0, n_blk, q_block, 0)


def _fox_prompt(qa, ka, va, l, lf, sel):
    b, seq, _ = qa.shape
    assert (seq - N_META) % ATT_BLK == 0
    full = lambda n: pl.BlockSpec((None, seq, n), lambda i: (i, 0, 0))
    layer = pl.BlockSpec((None, None, seq, D_A), lambda i: (l, i, 0, 0))
    heads = lambda: pltpu.VMEM((H_A, seq, LANES), BF16)
    return pl.pallas_call(
        functools.partial(_fox_prompt_kernel, seq=seq),
        grid=(b,),
        in_specs=[full(D_A), layer, layer, full(LANES),
                  _resident(sel.shape, lambda i: (0, 0, 0))],
        out_specs=full(D_A),
        out_shape=jax.ShapeDtypeStruct((b, seq, D_A), BF16),
        scratch_shapes=[pltpu.VMEM((seq, LANES), F32), heads(), heads(), heads()],
        compiler_params=_cparams(1),
        name="fox_prompt",
    )(qa, ka, va, lf, sel)


def _lane_cumsum_kernel(lf_ref, t_ref, c_ref):
    hi, mid, lo = _split3(lf_ref[...])
    t = t_ref[...]
    c_ref[...] = _dot(hi, t) + _dot(mid, t) + _dot(lo, t)


def _lane_cumsum(x, tri_u):
    rows = x.shape[0]
    tm = _row_tile(rows, 4096)
    spec = pl.BlockSpec((tm, LANES), lambda i: (i, 0))
    return pl.pallas_call(
        _lane_cumsum_kernel,
        grid=(rows // tm,),
        in_specs=[spec, _resident((LANES, LANES), lambda i: (0, 0))],
        out_specs=spec,
        out_shape=jax.ShapeDtypeStruct((rows, LANES), F32),
        compiler_params=_cparams(1),
        name="lane_cumsum",
    )(x, tri_u)


def _fox_sample_kernel(pt_ref, q_ref, *refs, n_steps, g_pages, n_tok):
    del pt_ref
    k_refs, v_refs, c_refs = refs[:g_pages], refs[g_pages:2 * g_pages], refs[2 * g_pages:3 * g_pages]
    kn_ref, vn_ref, lfn_ref, t_ref, o_ref, m_scr, l_scr, acc_scr, carry_scr = refs[3 * g_pages:]
    p = pl.program_id(1)
    page = k_refs[0].shape[-1]
    nq = q_ref.shape[0]
    q = q_ref[...]

    def update(s, pv):
        m_old = m_scr[...]
        m_new = jnp.maximum(m_old, jnp.max(s, axis=1, keepdims=True))
        alpha = jnp.exp(m_old - m_new)
        pr = jnp.exp(s - m_new)
        m_scr[...] = m_new
        l_scr[...] = alpha * l_scr[...] + jnp.sum(pr, axis=1, keepdims=True)
        acc_scr[...] = alpha * acc_scr[...] + pv(pr.astype(BF16))

    @pl.when(p == 0)
    def _():
        m_scr[...] = jnp.full(m_scr.shape, NEG, F32)
        l_scr[...] = jnp.zeros(l_scr.shape, F32)
        acc_scr[...] = jnp.zeros(acc_scr.shape, F32)
        carry_scr[...] = jnp.zeros(carry_scr.shape, F32)
        hi, mid, lo = _split3(jnp.broadcast_to(lfn_ref[...], (SUBLANES, LANES)))
        t = t_ref[...]
        c_new = (_dot(hi, t) + _dot(mid, t) + _dot(lo, t))[0:1, :nq]
        sn = _dot_nt(q, kn_ref[...]) - c_new
        rr = lax.broadcasted_iota(jnp.int32, sn.shape, 0)
        cc = lax.broadcasted_iota(jnp.int32, sn.shape, 1)
        ok = (rr // n_tok == cc // n_tok) & (cc % n_tok <= rr % n_tok)
        update(jnp.where(ok, sn, NEG), lambda pb: _dot(pb, vn_ref[...]))

    def per_head_rows(x):
        return jnp.concatenate([jnp.broadcast_to(x[h:h + 1, :], (n_tok, x.shape[1])) for h in range(H_A)], axis=0)

    carry = carry_scr[...]
    bias, ks, vs = [], [], []
    for g in range(g_pages):
        cin = c_refs[g][...]
        tot = cin[:, page - 1:page]
        bias.append(per_head_rows((tot - cin) + carry))
        carry = carry + tot
        ks.append(k_refs[g][...].reshape(D_A, page).astype(BF16))
        vs.append(v_refs[g][...].reshape(D_A, page).astype(BF16))
    carry_scr[...] = carry
    v_all = jnp.concatenate(vs, axis=1)
    s = _dot(q, jnp.concatenate(ks, axis=1)) + jnp.concatenate(bias, axis=1)
    update(s, lambda pb: _dot_nt(pb, v_all))

    @pl.when(p == n_steps - 1)
    def _():
        acc = acc_scr[...] / l_scr[...]
        o_ref[...] = jnp.concatenate(
            [acc[h * n_tok:(h + 1) * n_tok, h * DH_A:(h + 1) * DH_A] for h in range(H_A)], axis=0).astype(o_ref.dtype)


def _fox_sample(l, page_flat, q_hq, cache_kt, cache_vt, cpool, k_new, v_new, lf_new, tri_u, n_pages):
    n_seq, nq, _ = q_hq.shape
    page = cache_kt.shape[-1]
    g_pages = max(g for g in range(1, PAGES_PER_STEP + 1) if n_pages % g == 0)
    n_steps = n_pages // g_pages
    slot = lambda s, p, g: s * n_pages + (n_pages - 1 - (p * g_pages + g))
    idx = lambda g: (lambda s, p, pt: (l, pt[slot(s, p, g)], 0, 0, 0))
    cidx = lambda g: (lambda s, p, pt: (l, pt[slot(s, p, g)], 0, 0))
    kv_specs = [pl.BlockSpec((None, None, H_A, DH_A, page), idx(g)) for g in range(g_pages)]
    c_specs = [pl.BlockSpec((None, None, H_A, page), cidx(g)) for g in range(g_pages)]
    seq_spec = lambda a, b: pl.BlockSpec((None, a, b), lambda s, p, pt: (s, 0, 0))
    grid_spec = pltpu.PrefetchScalarGridSpec(
        num_scalar_prefetch=1,
        grid=(n_seq, n_steps),
        in_specs=[seq_spec(nq, D_A)] + kv_specs + kv_specs + c_specs
                 + [seq_spec(nq, D_A), seq_spec(nq, D_A), seq_spec(1, LANES),
                    pl.BlockSpec((LANES, LANES), lambda s, p, pt: (0, 0))],
        out_specs=seq_spec(nq, DH_A),
        scratch_shapes=[pltpu.VMEM((nq, 1), F32), pltpu.VMEM((nq, 1), F32),
                        pltpu.VMEM((nq, D_A), F32), pltpu.VMEM((H_A, 1), F32)],
    )
    return pl.pallas_call(
        functools.partial(_fox_sample_kernel, n_steps=n_steps, g_pages=g_pages, n_tok=nq // H_A),
        grid_spec=grid_spec,
        out_shape=jax.ShapeDtypeStruct((n_seq, nq, DH_A), BF16),
        compiler_params=_cparams(2),
        name="fox_sample",
    )(page_flat, q_hq, *([cache_kt] * g_pages), *([cache_vt] * g_pages), *([cpool] * g_pages),
      k_new, v_new, lf_new, tri_u)


CONV_PAD = 32


def _conv_kernel(u_ref, buf_ref, w_ref, b_ref, g_ref, beta_ref, act_ref, nbuf_ref, ext, dw, *, nb, seq, tr):
    n_tiles = -(-seq // tr)
    keep = CONV_W - 1
    for s in range(nb):
        ext[s, 0:CONV_PAD - keep, :] = jnp.zeros((CONV_PAD - keep, C_B), F32)
        ext[s, CONV_PAD - keep:CONV_PAD, :] = buf_ref[s]
        ext[s, CONV_PAD:CONV_PAD + seq, :] = u_ref[s]
        if n_tiles * tr > seq:
            ext[s, CONV_PAD + seq:CONV_PAD + n_tiles * tr, :] = jnp.zeros((n_tiles * tr - seq, C_B), F32)
        nbuf_ref[s] = ext[s, CONV_PAD + seq - keep:CONV_PAD + seq, :]

        def tile(i, _, s=s):
            r0 = pl.multiple_of(i * tr, SUBLANES)
            for c in range(C_B // LANES):
                win = ext[s, pl.ds(r0, tr + CONV_PAD), c * LANES:(c + 1) * LANES]
                shifted = [win] + [pltpu.roll(win, tr + CONV_PAD - sh, 0) for sh in range(1, SUBLANES)]
                acc = jnp.broadcast_to(b_ref[:, c * LANES:(c + 1) * LANES], (tr, LANES))
                for j in range(CONV_W):
                    a, sh = divmod(CONV_PAD - keep + j, SUBLANES)
                    acc = acc + shifted[sh][a * SUBLANES:a * SUBLANES + tr, :] * w_ref[j:j + 1, c * LANES:(c + 1) * LANES]
                dw[pl.ds(r0, tr), c * LANES:(c + 1) * LANES] = acc
            return 0

        lax.fori_loop(0, n_tiles, tile, 0)

        def norm(r0, n, s=s):
            x = dw[pl.ds(r0, n), :]
            mu = jnp.mean(x, axis=-1, keepdims=True)
            xc = x - mu
            var = jnp.mean(xc * xc, axis=-1, keepdims=True)
            y = xc * lax.rsqrt(var + EPS) * g_ref[...] + beta_ref[...]
            act_ref[s, pl.ds(r0, n), :] = (y * _sigmoid(y)).astype(act_ref.dtype)

        n_full = seq // tr

        def norm_tile(i, _):
            norm(pl.multiple_of(i * tr, SUBLANES), tr)
            return 0

        lax.fori_loop(0, n_full, norm_tile, 0)
        if seq > n_full * tr:
            norm(n_full * tr, seq - n_full * tr)


def _conv(u, buf, l, w_dw, b_dw, ln_g, ln_b, nb):
    bsz, seq, _ = u.shape
    tr = LANES if seq >= LANES else seq
    n_tiles = -(-seq // tr)
    keep = CONV_W - 1
    par = lambda r: pl.BlockSpec((None, r, C_B), lambda i: (l, 0, 0))
    if buf.ndim == 4:
        buf_spec = pl.BlockSpec((None, nb, keep, C_B), lambda i: (l, i, 0, 0))
    else:
        buf_spec = pl.BlockSpec((nb, keep, C_B), lambda i: (i, 0, 0))
    return pl.pallas_call(
        functools.partial(_conv_kernel, nb=nb, seq=seq, tr=tr),
        grid=(bsz // nb,),
        in_specs=[pl.BlockSpec((nb, seq, C_B), lambda i: (i, 0, 0)), buf_spec,
                  par(CONV_PAD), par(1), par(1), par(1)],
        out_specs=[pl.BlockSpec((nb, seq, C_B), lambda i: (i, 0, 0)),
                   pl.BlockSpec((nb, keep, C_B), lambda i: (i, 0, 0))],
        out_shape=[jax.ShapeDtypeStruct((bsz, seq, C_B), BF16),
                   jax.ShapeDtypeStruct((bsz, keep, C_B), F32)],
        scratch_shapes=[pltpu.VMEM((nb, CONV_PAD + n_tiles * tr, C_B), F32),
                        pltpu.VMEM((n_tiles * tr, C_B), F32)],
        compiler_params=_cparams(1),
        name="conv",
    )(u, buf, w_dw, b_dw, ln_g, ln_b)


def _ret_chunk(s, q, k, v, n, lg):
    ii = lax.broadcasted_iota(jnp.int32, (n, n), 0)
    jj = lax.broadcasted_iota(jnp.int32, (n, n), 1)
    rel = (ii - jj).astype(F32)
    decay = jnp.where(rel >= 0, jnp.exp(jnp.maximum(rel, 0.0) * lg), 0.0)
    scores = _dot_nt(q, k.astype(BF16)) * decay
    o = _dot(scores.astype(BF16), v)
    jc = lax.broadcasted_iota(jnp.int32, (n, 1), 0).astype(F32)
    o = o + _dot(q, s.astype(BF16)) * jnp.exp((jc + 1.0) * lg)
    ks = (k * jnp.exp((n - 1.0 - jc) * lg)).astype(BF16)
    kv = lax.dot_general(ks, v, (((0,), (0,)), ((), ())), preferred_element_type=F32)
    gamma_n = float(np.exp(n * lg)) if isinstance(lg, float) else jnp.exp(n * lg)
    return gamma_n * s + kv, o


def _group_norm(o, g):
    mu = jnp.mean(o, axis=-1, keepdims=True)
    oc = o - mu
    var = jnp.mean(oc * oc, axis=-1, keepdims=True)
    return oc * lax.rsqrt(var + EPS) * g


def _head_log_gamma(h):
    hv = jnp.full((1, 1), h, jnp.int32)
    lg = jnp.full((1, 1), _LOG_GAMMA[H_C - 1], F32)
    for i in range(H_C - 2, -1, -1):
        lg = jnp.where(hv == i, jnp.float32(_LOG_GAMMA[i]), lg)
    return lg


def _ret_prompt_kernel(q_ref, k_ref, v_ref, g_ref, *rest, seq):
    o_ref, s_ref = rest[-2:]

    def chunk(r0, n, states):
        qs = [q_ref[pl.ds(r0, n), h * DK_C:(h + 1) * DK_C] for h in range(H_C)]
        ks = [k_ref[pl.ds(r0, n), h * DK_C:(h + 1) * DK_C] for h in range(H_C)]
        vs = [v_ref[pl.ds(r0, n), h * DV_C:(h + 1) * DV_C] for h in range(H_C)]
        new, outs = [], []
        for h in range(H_C):
            s, o = _ret_chunk(states[h], qs[h], ks[h], vs[h], n, _LOG_GAMMA[h])
            new.append(s)
            outs.append(_group_norm(o, g_ref[:, h * DV_C:(h + 1) * DV_C]))
        o_ref[pl.ds(r0, n), :] = jnp.concatenate(outs, axis=1).astype(o_ref.dtype)
        return tuple(new)

    states = chunk(0, N_META, tuple(jnp.zeros((DK_C, DV_C), F32) for _ in range(H_C)))
    n_chunks = (seq - N_META) // RET_BLK
    states = lax.fori_loop(0, n_chunks,
                           lambda i, st: chunk(pl.multiple_of(N_META + i * RET_BLK, N_META), RET_BLK, st), states,
                           unroll=2 if n_chunks % 2 == 0 else 1)
    for h in range(H_C):
        s_ref[h] = states[h]


def _ret_prompt(qr, kr, vc, l, gn_g, s_prev):
    b, seq, _ = qr.shape
    depth = gn_g.shape[0]
    assert (seq - N_META) % RET_BLK == 0
    full = lambda n: pl.BlockSpec((None, seq, n), lambda i: (i, 0, 0))
    s_spec, s_shape, extra = _stacked_out((depth, b, H_C, DK_C, DV_C), (None, H_C, DK_C, DV_C),
                                          lambda i: (i, 0, 0, 0), l, s_prev)
    return pl.pallas_call(
        functools.partial(_ret_prompt_kernel, seq=seq),
        grid=(b,),
        in_specs=[full(D_QK_C), full(D_QK_C), full(D_V_C), pl.BlockSpec((None, 1, D_V_C), lambda i: (l, 0, 0))]
                 + [s for s, _ in extra],
        out_specs=[full(D_V_C), s_spec],
        out_shape=[jax.ShapeDtypeStruct((b, seq, D_V_C), BF16), s_shape],
        input_output_aliases={4: 1} if extra else {},
        compiler_params=_cparams(1),
        name="ret_prompt",
    )(qr, kr, vc, gn_g, *[a for _, a in extra])


def _ret_sample_kernel(q_ref, k_ref, v_ref, g_ref, s0_ref, *rest, nb, n_tok):
    o_ref, s_ref = rest[-2:]
    lg = _head_log_gamma(pl.program_id(1))
    g = g_ref[...]
    for s in range(nb):
        s_new, o = _ret_chunk(s0_ref[s], q_ref[s], k_ref[s], v_ref[s], n_tok, lg)
        o_ref[s] = _group_norm(o, g).astype(o_ref.dtype)
        s_ref[s] = s_new


def _ret_sample(qr, kr, vc, l, gn_g, state, nb, s_prev):
    n_seq, n_tok, _ = qr.shape
    qk = pl.BlockSpec((nb, n_tok, DK_C), lambda i, h: (i, 0, h))
    vv = pl.BlockSpec((nb, n_tok, DV_C), lambda i, h: (i, 0, h))
    s_spec, s_shape, extra = _stacked_out(state.shape, (nb, None, DK_C, DV_C), lambda i, h: (i, h, 0, 0), l, s_prev)
    return pl.pallas_call(
        functools.partial(_ret_sample_kernel, nb=nb, n_tok=n_tok),
        grid=(n_seq // nb, H_C),
        in_specs=[qk, qk, vv, pl.BlockSpec((None, 1, DV_C), lambda i, h: (l, 0, h)),
                  pl.BlockSpec((None, nb, None, DK_C, DV_C), lambda i, h: (l, i, h, 0, 0))] + [s for s, _ in extra],
        out_specs=[vv, s_spec],
        out_shape=[jax.ShapeDtypeStruct((n_seq, n_tok, D_V_C), BF16), s_shape],
        input_output_aliases={5: 1} if extra else {},
        compiler_params=_cparams(2),
        name="ret_sample",
    )(qr, kr, vc, gn_g, state, *[a for _, a in extra])


def _merge_kernel(x_ref, oa_ref, cb_ref, on_ref, gpre_ref, gpost_ref, wgc_ref, wgate_ref, wao_ref,
                  wpw_ref, wro_ref, wo_ref, y_ref):
    x = x_ref[...]
    h = _rms(x, gpre_ref[...]).astype(BF16)
    zc = _dot(h, wgc_ref[...])
    y_c = _dot((zc * _sigmoid(zc) * on_ref[...]).astype(BF16), wro_ref[...])
    y_a = _dot(oa_ref[...], wao_ref[...])
    y_b = _dot(cb_ref[...], wpw_ref[...])
    merged = (_sigmoid(_dot(h, wgate_ref[:, 0:D_MODEL])) * y_a
              + _sigmoid(_dot(h, wgate_ref[:, D_MODEL:2 * D_MODEL])) * y_b
              + _sigmoid(_dot(h, wgate_ref[:, 2 * D_MODEL:])) * y_c)
    out = _dot(merged.astype(BF16), wo_ref[...])
    y_ref[...] = x + _rms(out, gpost_ref[...])


def _merge(x, oa, cb, on, l, gpre, gpost, wgc, wgate, wao, wpw, wro, wo, tm):
    t = x.shape[0]
    row = lambda n: pl.BlockSpec((tm, n), lambda i: (i, 0))
    w = lambda k, n: _resident((None, k, n), lambda i: (l, 0, 0))
    return pl.pallas_call(
        _merge_kernel,
        grid=(t // tm,),
        in_specs=[row(D_MODEL), row(D_A), row(C_B), row(D_V_C), w(1, D_MODEL), w(1, D_MODEL),
                  w(D_MODEL, D_V_C), w(D_MODEL, N_BRANCH * D_MODEL), w(D_A, D_MODEL), w(C_B, D_MODEL),
                  w(D_V_C, D_MODEL), w(D_MODEL, D_MODEL)],
        out_specs=row(D_MODEL),
        out_shape=jax.ShapeDtypeStruct((t, D_MODEL), F32),
        compiler_params=_cparams(1),
        name="merge",
    )(x, oa, cb, on, gpre, gpost, wgc, wgate, wao, wpw, wro, wo)


FF_CHUNK = 1024


def _ffn_kernel(x_ref, gpre_ref, gpost_ref, wup_ref, wdn_ref, y_ref):
    x = x_ref[...]
    h = _rms(x, gpre_ref[...]).astype(BF16)
    ff = jnp.zeros(x.shape, F32)
    for c in range(D_FF // FF_CHUNK):
        up = jnp.maximum(_dot(h, wup_ref[:, c * FF_CHUNK:(c + 1) * FF_CHUNK]), 0.0)
        ff = ff + _dot((up * up).astype(BF16), wdn_ref[c * FF_CHUNK:(c + 1) * FF_CHUNK, :])
    y_ref[...] = x + _rms(ff, gpost_ref[...])


def _ffn(x, l, gpre, gpost, wup, wdn, tm):
    t = x.shape[0]
    row = pl.BlockSpec((tm, D_MODEL), lambda i: (i, 0))
    w = lambda k, n: _resident((None, k, n), lambda i: (l, 0, 0))
    return pl.pallas_call(
        _ffn_kernel,
        grid=(t // tm,),
        in_specs=[row, w(1, D_MODEL), w(1, D_MODEL), w(D_MODEL, D_FF), w(D_FF, D_MODEL)],
        out_specs=row,
        out_shape=jax.ShapeDtypeStruct((t, D_MODEL), F32),
        compiler_params=_cparams(1),
        name="ffn",
    )(x, gpre, gpost, wup, wdn)


def _rope_tables(pos):
    half = DK_C // 2
    inv = ROPE_BASE ** (-jnp.arange(half, dtype=F32) / half)
    ang = pos.astype(F32)[:, None] * inv[None, :]
    cos, sin = jnp.cos(ang), jnp.sin(ang)
    return jnp.concatenate([cos, cos], axis=1), jnp.concatenate([-sin, sin], axis=1)


def kernel(x_prompt, x_sample, cache_k, cache_v, cache_logf, state_conv, state_ret, page_table,
           meta_tokens, norm_mix_pre, norm_mix_post, norm_ffn_pre, norm_ffn_post,
           w_in, b_forget, w_dw, b_dw, ln_conv_g, ln_conv_b, w_pw_out, w_attn_out,
           gn_ret_g, w_ret_out, w_o, w_ff_up, w_ff_down):
    depth = w_in.shape[0]
    bp, seq_p, _ = x_prompt.shape
    n_dec, n_tok, _ = x_sample.shape
    n_pages = page_table.shape[1]
    n_pool, page = cache_k.shape[1], cache_k.shape[2]
    assert page == LANES
    past_len = n_pages * page
    lp = N_META + seq_p
    tp, ts = bp * lp, n_dec * n_tok

    cuts = np.cumsum([0, D_A, D_A, D_A, H_A, 2 * C_B, D_QK_C, D_QK_C, D_V_C, D_V_C, N_BRANCH * D_MODEL])
    seg = lambda a, b: w_in[:, :, int(cuts[a]):int(cuts[b])].astype(BF16)
    wa, wglu, wqk, wv, wgc, wgate = seg(0, 3), seg(4, 5), seg(5, 7), seg(7, 8), seg(8, 9), seg(9, 10)
    wf = jnp.pad(w_in[:, :, int(cuts[3]):int(cuts[4])], ((0, 0), (0, 0), (0, LANES - H_A))).astype(BF16)
    bfp = jnp.pad(b_forget, ((0, 0), (0, LANES - H_A)))[:, None, :]
    wao, wpw, wro, wo = (w.astype(BF16) for w in (w_attn_out, w_pw_out, w_ret_out, w_o))
    wup, wdn = w_ff_up.astype(BF16), w_ff_down.astype(BF16)
    vec = lambda a: a[:, None, :]
    g_mpre, g_mpost, g_fpre, g_fpost = vec(norm_mix_pre), vec(norm_mix_post), vec(norm_ffn_pre), vec(norm_ffn_post)
    wdw = jnp.pad(w_dw, ((0, 0), (0, CONV_PAD - CONV_W), (0, 0)))
    bdw, lng, lnb, gng = vec(b_dw), vec(ln_conv_g), vec(ln_conv_b), vec(gn_ret_g)

    tm_in_p = _row_tile(tp, 700, must_divide=lp)
    tm_in_s = _row_tile(ts, 256)
    tm_p = _row_tile(tp, 400)
    tm_ffn_p = _row_tile(tp, 700)
    tm_s = _row_tile(ts, 512)
    cos_p, sin_p = _rope_tables(jnp.arange(lp))
    cos_s, sin_s = _rope_tables(past_len + jnp.arange(n_tok))
    cos_s, sin_s = (jnp.tile(a, (tm_in_s // n_tok, 1)) for a in (cos_s, sin_s))
    sel = _fox_aug_select()

    tri_u = jnp.asarray(np.triu(np.ones((LANES, LANES), np.float32)), dtype=BF16)
    cache_kt = cache_k.transpose(0, 1, 3, 4, 2)
    cache_vt = cache_v.transpose(0, 1, 3, 4, 2)
    lf_t = cache_logf.transpose(0, 1, 3, 2).reshape(depth * n_pool * H_A, page)
    cpool = _lane_cumsum(lf_t, tri_u).reshape(depth, n_pool, H_A, page)
    page_flat = page_table.reshape(-1)

    xp = jnp.concatenate([jnp.broadcast_to(meta_tokens[None], (bp, N_META, D_MODEL)), x_prompt], axis=1)
    xp = xp.reshape(tp, D_MODEL)
    xs = x_sample.reshape(ts, D_MODEL)
    buf0 = jnp.zeros((bp, CONV_W - 1, C_B), F32)
    nq = H_A * n_tok
    assert nq <= LANES
    own_head = jnp.asarray(np.arange(nq)[:, None] // n_tok == np.arange(D_A)[None, :] // DH_A)
    lane_grp = np.arange(LANES)
    t_grp = jnp.asarray(((lane_grp[:, None] // n_tok == lane_grp[None, :] // n_tok)
                         & (lane_grp[:, None] <= lane_grp[None, :])).astype(np.float32), dtype=BF16)

    def head_major(a):
        a = jnp.broadcast_to(a.reshape(n_dec, 1, n_tok, D_A), (n_dec, H_A, n_tok, D_A)).reshape(n_dec, nq, D_A)
        return jnp.where(own_head, a, 0).astype(BF16)

    p_lf, p_conv, s_lf, s_conv = [], [], [], []
    pk = pv = p_ret = sk = sv = s_ret = None
    for l in range(depth):
        qa, pk, pv, lf, u, qr, kr, vc = _inproj(xp, l, g_mpre, wa, wf, bfp, wglu, wqk, wv, cos_p, sin_p,
                                                tm_in_p, pk, pv)
        r3 = lambda a: a.reshape(bp, lp, a.shape[-1])
        oa = _fox_prompt(r3(qa), pk.reshape(depth, bp, lp, D_A), pv.reshape(depth, bp, lp, D_A), l, r3(lf), sel)
        cact, nbuf = _conv(r3(u), buf0, l, wdw, bdw, lng, lnb, 1)
        on, p_ret = _ret_prompt(r3(qr), r3(kr), r3(vc), l, gng, p_ret)
        xp = _merge(xp, oa.reshape(tp, D_A), cact.reshape(tp, C_B), on.reshape(tp, D_V_C), l,
                    g_mpre, g_mpost, wgc, wgate, wao, wpw, wro, wo, tm_p)
        xp = _ffn(xp, l, g_fpre, g_fpost, wup, wdn, tm_ffn_p)
        p_lf.append(lf[:, :H_A].reshape(bp, lp, H_A))
        p_conv.append(nbuf)

        qa, sk, sv, lf, u, qr, kr, vc = _inproj(xs, l, g_mpre, wa, wf, bfp, wglu, wqk, wv, cos_s, sin_s,
                                                tm_in_s, sk, sv)
        s3 = lambda a: a.reshape(n_dec, n_tok, a.shape[-1])
        lf8 = lf[:, :H_A]
        lf_new = jnp.pad(lf8.reshape(n_dec, n_tok, H_A).transpose(0, 2, 1).reshape(n_dec, 1, nq),
                         ((0, 0), (0, 0), (0, LANES - nq)))
        o_hq = _fox_sample(l, page_flat, head_major(qa), cache_kt, cache_vt, cpool,
                           head_major(sk[l]), head_major(sv[l]), lf_new, t_grp, n_pages)
        oa = o_hq.reshape(n_dec, H_A, n_tok, DH_A).transpose(0, 2, 1, 3).reshape(ts, D_A)
        cact, nbuf = _conv(s3(u), state_conv, l, wdw, bdw, lng, lnb, 16)
        on, s_ret = _ret_sample(s3(qr), s3(kr), s3(vc), l, gng, state_ret, 8, s_ret)
        xs = _merge(xs, oa, cact.reshape(ts, C_B), on.reshape(ts, D_V_C), l,
                    g_mpre, g_mpost, wgc, wgate, wao, wpw, wro, wo, tm_s)
        xs = _ffn(xs, l, g_fpre, g_fpost, wup, wdn, tm_s)
        s_lf.append(lf8.reshape(n_dec, n_tok, H_A))
        s_conv.append(nbuf)

    y_prompt = xp.reshape(bp, lp, D_MODEL)[:, N_META:]
    y_sample = xs.reshape(n_dec, n_tok, D_MODEL)
    return (y_prompt, y_sample,
            pk.reshape(depth, bp, lp, H_A, DH_A), pv.reshape(depth, bp, lp, H_A, DH_A),
            jnp.stack(p_lf), jnp.stack(p_conv), p_ret,
            sk.reshape(depth, n_dec, n_tok, H_A, DH_A), sv.reshape(depth, n_dec, n_tok, H_A, DH_A),
            jnp.stack(s_lf), jnp.stack(s_conv), s_ret)
```

```python
import functools

import numpy as np
import jax
import jax.numpy as jnp
from jax import lax
from jax.experimental import pallas as pl
from jax.experimental.pallas import tpu as pltpu

D_MODEL = 1024
N_META = 16
H_A = 8
DH_A = 64
D_A = H_A * DH_A
C_B = 512
CONV_W = 31
H_C = 4
DK_C = 128
DV_C = 256
D_QK_C = H_C * DK_C
D_V_C = H_C * DV_C
N_BRANCH = 3
D_FF = 4 * D_MODEL
EPS = 1e-6
ROPE_BASE = 10000.0

LANES = 128
SUBLANES = 8
ATT_BLK = 512
RET_BLK = 128
NEG = -1e30
VMEM_LIMIT = 56 * 1024 * 1024
PAGES_PER_STEP = 16

BF16 = jnp.bfloat16
F32 = jnp.float32

_LOG_GAMMA = tuple(float(np.log1p(-(2.0 ** (-5.0 - h)))) for h in range(H_C))


def _cparams(n_axes):
    return pltpu.CompilerParams(dimension_semantics=("arbitrary",) * n_axes,
                                vmem_limit_bytes=VMEM_LIMIT)


def _row_tile(total, target, must_divide=None):
    best = None
    for t in range(SUBLANES, min(total, target) + 1, SUBLANES):
        if total % t == 0 and (must_divide is None or must_divide % t == 0):
            best = t
    assert best is not None, (total, target, must_divide)
    return best


def _resident(shape, index_map):
    return pl.BlockSpec(shape, index_map, pipeline_mode=pl.Buffered(1))


def _dot(a, b):
    return jnp.dot(a, b, preferred_element_type=F32)


def _dot_nt(a, b):
    return lax.dot_general(a, b, (((1,), (1,)), ((), ())), preferred_element_type=F32)


def _split3(x):
    hi = x.astype(BF16)
    r1 = x - hi.astype(F32)
    mid = r1.astype(BF16)
    r2 = r1 - mid.astype(F32)
    return hi, mid, r2.astype(BF16)


def _rms(x, g):
    return x * lax.rsqrt(jnp.mean(x * x, axis=-1, keepdims=True) + EPS) * g


def _log_sigmoid(z):
    return jnp.minimum(z, 0.0) - jnp.log1p(jnp.exp(-jnp.abs(z)))


def _sigmoid(z):
    return 1.0 / (1.0 + jnp.exp(-z))


def _stacked_out(shape, block, index_map, l, prev):
    spec = pl.BlockSpec((None,) + block, lambda *i: (l,) + index_map(*i))
    extra_in = [] if prev is None else [(pl.BlockSpec(memory_space=pl.ANY), prev)]
    return spec, jax.ShapeDtypeStruct(shape, F32), extra_in


def _inproj_kernel(x_ref, g_ref, wa_ref, wf_ref, bf_ref, wglu_ref, wqk_ref, wv_ref, cos_ref, sin_ref, *rest):
    qa_ref, ka_ref, va_ref, lf_ref, u_ref, qr_ref, kr_ref, vc_ref = rest[-8:]
    h = _rms(x_ref[...], g_ref[...]).astype(BF16)
    za = _dot(h, wa_ref[...])
    qa_ref[...] = (za[:, :D_A] * (DH_A ** -0.5)).astype(BF16)
    ka_ref[...] = za[:, D_A:2 * D_A]
    va_ref[...] = za[:, 2 * D_A:]
    lf_ref[...] = _log_sigmoid(_dot(h, wf_ref[...]) + bf_ref[...])
    zg = _dot(h, wglu_ref[...])
    u_ref[...] = zg[:, :C_B] * _sigmoid(zg[:, C_B:])
    zqk = _dot(h, wqk_ref[...])
    cos = cos_ref[...]
    sin = sin_ref[...]
    for j in range(2 * H_C):
        xh = zqk[:, j * DK_C:(j + 1) * DK_C]
        r = xh * cos + pltpu.roll(xh, DK_C // 2, 1) * sin
        if j < H_C:
            qr_ref[:, j * DK_C:(j + 1) * DK_C] = r.astype(BF16)
        else:
            kr_ref[:, (j - H_C) * DK_C:(j - H_C + 1) * DK_C] = r * (DK_C ** -0.5)
    vc_ref[...] = _dot(h, wv_ref[...]).astype(BF16)


def _inproj(x, l, gpre, wa, wf, bfp, wglu, wqk, wv, cos_t, sin_t, tm, k_prev, v_prev):
    t = x.shape[0]
    depth = wa.shape[0]
    n_tab = cos_t.shape[0] // tm
    row = lambda n: pl.BlockSpec((tm, n), lambda i: (i, 0))
    wspec = lambda n: _resident((None, D_MODEL, n), lambda i: (l, 0, 0))
    tab = pl.BlockSpec((tm, LANES), lambda i: (i % n_tab, 0))
    outs = [(D_A, BF16), None, None, (LANES, F32), (C_B, F32), (D_QK_C, BF16), (D_QK_C, F32), (D_V_C, BF16)]
    out_specs = [None if o is None else row(o[0]) for o in outs]
    out_shape = [None if o is None else jax.ShapeDtypeStruct((t, o[0]), o[1]) for o in outs]
    in_specs = [row(D_MODEL), _resident((None, 1, D_MODEL), lambda i: (l, 0, 0)),
                wspec(3 * D_A), wspec(LANES), _resident((None, 1, LANES), lambda i: (l, 0, 0)),
                wspec(2 * C_B), wspec(2 * D_QK_C), wspec(D_V_C), tab, tab]
    args = [x, gpre, wa, wf, bfp, wglu, wqk, wv, cos_t, sin_t]
    aliases = {}
    for o, prev in ((1, k_prev), (2, v_prev)):
        out_specs[o], out_shape[o], extra = _stacked_out((depth, t, D_A), (tm, D_A), lambda i: (i, 0), l, prev)
        for spec, arr in extra:
            aliases[len(args)] = o
            in_specs.append(spec)
            args.append(arr)
    return pl.pallas_call(
        _inproj_kernel,
        grid=(t // tm,),
        in_specs=in_specs,
        out_specs=out_specs,
        out_shape=out_shape,
        input_output_aliases=aliases,
        compiler_params=_cparams(1),
        name="inproj",
    )(*args)


N_AUG = 3
PACK_ONE = N_AUG * H_A


def _fox_aug_select():
    sel = np.zeros((H_A // 2, LANES, 2 * LANES), np.float32)
    for p in range(H_A // 2):
        for hh in range(2):
            h, x0 = 2 * p + hh, (1 - hh) * DH_A
            qb, kb = x0, LANES + x0
            for j in range(N_AUG):
                sel[p, j * H_A + h, qb + j] = 1.0
                sel[p, PACK_ONE, qb + N_AUG + j] = 1.0
                sel[p, PACK_ONE, kb + j] = 1.0
                sel[p, j * H_A + h, kb + N_AUG + j] = -1.0
    return jnp.asarray(sel, dtype=BF16)


def _fox_prompt_kernel(q_ref, k_ref, v_ref, lf_ref, sel_ref, o_ref, c_scr, qaug, kaug, vaug, *, seq):
    n_full = seq // LANES
    tail = seq - n_full * LANES
    n_blk = (seq - N_META) // ATT_BLK

    r = lax.broadcasted_iota(jnp.int32, (LANES, LANES), 0)
    c = lax.broadcasted_iota(jnp.int32, (LANES, LANES), 1)
    tri = jnp.where(c <= r, 1.0, 0.0).astype(BF16)
    carry = jnp.zeros((1, LANES), F32)
    for b in range(n_full + (1 if tail else 0)):
        nb = LANES if b < n_full else tail
        hi, mid, lo = _split3(lf_ref[b * LANES:b * LANES + nb, :])
        tb = tri[:nb, :nb]
        cb = _dot(tb, hi) + _dot(tb, mid) + _dot(tb, lo) + carry
        c_scr[b * LANES:b * LANES + nb, :] = cb
        carry = cb[nb - 1:nb, :]

    lane = lax.broadcasted_iota(jnp.int32, (seq, LANES), 1)
    cc = c_scr[...]
    hi = cc.astype(BF16).astype(F32)
    r1 = cc - hi
    mid = r1.astype(BF16).astype(F32)
    lo = r1 - mid
    packed = jnp.where(lane < H_A, hi,
                       jnp.where(lane < 2 * H_A, pltpu.roll(mid, H_A, 1),
                                 jnp.where(lane < PACK_ONE, pltpu.roll(lo, 2 * H_A, 1),
                                           jnp.where(lane == PACK_ONE, 1.0, 0.0)))).astype(BF16)
    for p in range(H_A // 2):
        aug = _dot(packed, sel_ref[p])
        qp = q_ref[:, p * LANES:(p + 1) * LANES].astype(F32)
        kp = k_ref[:, p * LANES:(p + 1) * LANES]
        vp = v_ref[:, p * LANES:(p + 1) * LANES]
        for hh in range(2):
            h, x0 = 2 * p + hh, (1 - hh) * DH_A
            own = (lane >= hh * DH_A) & (lane < (hh + 1) * DH_A)
            qaug[h] = jnp.where(own, qp, aug[:, :LANES]).astype(BF16)
            kaug[h] = jnp.where(own, kp, aug[:, LANES:]).astype(BF16)
            vaug[h] = jnp.where(own, vp, jnp.where(lane == x0, 1.0, 0.0)).astype(BF16)

    def first(h, qa, nq):
        s = _dot_nt(qa, kaug[h, 0:N_META, :])
        if nq == N_META:
            rr = lax.broadcasted_iota(jnp.int32, s.shape, 0)
            cc2 = lax.broadcasted_iota(jnp.int32, s.shape, 1)
            s = jnp.where(rr >= cc2, s, NEG)
        m = jnp.max(s, axis=1, keepdims=True)
        acc = _dot(jnp.exp(s - m).astype(BF16), vaug[h, 0:N_META, :])
        return m, acc

    def steps(q0, start, carry, causal):
        qs = [qaug[h, pl.ds(q0, ATT_BLK), :] for h in range(H_A)]
        ks = [kaug[h, pl.ds(start, ATT_BLK), :] for h in range(H_A)]
        vs = [vaug[h, pl.ds(start, ATT_BLK), :] for h in range(H_A)]
        new = []
        for h in range(H_A):
            m_old, acc = carry[h]
            s = _dot_nt(qs[h], ks[h])
            if causal:
                rr = lax.broadcasted_iota(jnp.int32, s.shape, 0)
                cc2 = lax.broadcasted_iota(jnp.int32, s.shape, 1)
                s = jnp.where(rr >= cc2, s, NEG)
            m_new = jnp.maximum(m_old, jnp.max(s, axis=1, keepdims=True))
            pr = jnp.exp(s - m_new).astype(BF16)
            new.append((m_new, jnp.exp(m_old - m_new) * acc + _dot(pr, vs[h])))
        return tuple(new)

    def normalised(acc0, acc1, nq):
        ln = lax.broadcasted_iota(jnp.int32, (nq, LANES), 1)
        return jnp.where(ln < DH_A, acc0 / acc0[:, DH_A:DH_A + 1], acc1 / acc1[:, 0:1])

    for p in range(H_A // 2):
        a0 = first(2 * p, qaug[2 * p, 0:N_META, :], N_META)[1]
        a1 = first(2 * p + 1, qaug[2 * p + 1, 0:N_META, :], N_META)[1]
        o_ref[0:N_META, p * LANES:(p + 1) * LANES] = normalised(a0, a1, N_META).astype(o_ref.dtype)

    def q_block(i, _):
        q0 = pl.multiple_of(N_META + i * ATT_BLK, N_META)
        carry = tuple(first(h, qaug[h, pl.ds(q0, ATT_BLK), :], ATT_BLK) for h in range(H_A))

        def kv_block(j, carry):
            return steps(q0, pl.multiple_of(N_META + j * ATT_BLK, N_META), carry, False)

        carry = lax.fori_loop(0, i, kv_block, carry)
        carry = steps(q0, q0, carry, True)
        for p in range(H_A // 2):
            o_ref[pl.ds(q0, ATT_BLK), p * LANES:(p + 1) * LANES] = normalised(
                carry[2 * p][1], carry[2 * p + 1][1], ATT_BLK).astype(o_ref.dtype)
        return 0

    lax.fori_loop(0, n_blk, q_block, 0)


def _fox_prompt(qa, ka, va, l, lf, sel):
    b, seq, _ = qa.shape
    assert (seq - N_META) % ATT_BLK == 0
    full = lambda n: pl.BlockSpec((None, seq, n), lambda i: (i, 0, 0))
    layer = pl.BlockSpec((None, None, seq, D_A), lambda i: (l, i, 0, 0))
    heads = lambda: pltpu.VMEM((H_A, seq, LANES), BF16)
    return pl.pallas_call(
        functools.partial(_fox_prompt_kernel, seq=seq),
        grid=(b,),
        in_specs=[full(D_A), layer, layer, full(LANES),
                  _resident(sel.shape, lambda i: (0, 0, 0))],
        out_specs=full(D_A),
        out_shape=jax.ShapeDtypeStruct((b, seq, D_A), BF16),
        scratch_shapes=[pltpu.VMEM((seq, LANES), F32), heads(), heads(), heads()],
        compiler_params=_cparams(1),
        name="fox_prompt",
    )(qa, ka, va, lf, sel)


def _lane_cumsum_kernel(lf_ref, t_ref, c_ref):
    hi, mid, lo = _split3(lf_ref[...])
    t = t_ref[...]
    c_ref[...] = _dot(hi, t) + _dot(mid, t) + _dot(lo, t)


def _lane_cumsum(x, tri_u):
    rows = x.shape[0]
    tm = _row_tile(rows, 4096)
    spec = pl.BlockSpec((tm, LANES), lambda i: (i, 0))
    return pl.pallas_call(
        _lane_cumsum_kernel,
        grid=(rows // tm,),
        in_specs=[spec, _resident((LANES, LANES), lambda i: (0, 0))],
        out_specs=spec,
        out_shape=jax.ShapeDtypeStruct((rows, LANES), F32),
        compiler_params=_cparams(1),
        name="lane_cumsum",
    )(x, tri_u)


def _fox_sample_kernel(pt_ref, q_ref, *refs, n_steps, g_pages, n_tok):
    del pt_ref
    k_refs, v_refs, c_refs = refs[:g_pages], refs[g_pages:2 * g_pages], refs[2 * g_pages:3 * g_pages]
    kn_ref, vn_ref, lfn_ref, t_ref, o_ref, m_scr, l_scr, acc_scr, carry_scr = refs[3 * g_pages:]
    p = pl.program_id(1)
    page = k_refs[0].shape[-1]
    nq = q_ref.shape[0]
    q = q_ref[...]

    def update(s, pv):
        m_old = m_scr[...]
        m_new = jnp.maximum(m_old, jnp.max(s, axis=1, keepdims=True))
        alpha = jnp.exp(m_old - m_new)
        pr = jnp.exp(s - m_new)
        m_scr[...] = m_new
        l_scr[...] = alpha * l_scr[...] + jnp.sum(pr, axis=1, keepdims=True)
        acc_scr[...] = alpha * acc_scr[...] + pv(pr.astype(BF16))

    @pl.when(p == 0)
    def _():
        m_scr[...] = jnp.full(m_scr.shape, NEG, F32)
        l_scr[...] = jnp.zeros(l_scr.shape, F32)
        acc_scr[...] = jnp.zeros(acc_scr.shape, F32)
        carry_scr[...] = jnp.zeros(carry_scr.shape, F32)
        hi, mid, lo = _split3(jnp.broadcast_to(lfn_ref[...], (SUBLANES, LANES)))
        t = t_ref[...]
        c_new = (_dot(hi, t) + _dot(mid, t) + _dot(lo, t))[0:1, :nq]
        sn = _dot_nt(q, kn_ref[...]) - c_new
        rr = lax.broadcasted_iota(jnp.int32, sn.shape, 0)
        cc = lax.broadcasted_iota(jnp.int32, sn.shape, 1)
        ok = (rr // n_tok == cc // n_tok) & (cc % n_tok <= rr % n_tok)
        update(jnp.where(ok, sn, NEG), lambda pb: _dot(pb, vn_ref[...]))

    def per_head_rows(x):
        return jnp.concatenate([jnp.broadcast_to(x[h:h + 1, :], (n_tok, x.shape[1])) for h in range(H_A)], axis=0)

    carry = carry_scr[...]
    bias, ks, vs = [], [], []
    for g in range(g_pages):
        cin = c_refs[g][...]
        tot = cin[:, page - 1:page]
        bias.append(per_head_rows((tot - cin) + carry))
        carry = carry + tot
        ks.append(k_refs[g][...].reshape(D_A, page).astype(BF16))
        vs.append(v_refs[g][...].reshape(D_A, page).astype(BF16))
    carry_scr[...] = carry
    v_all = jnp.concatenate(vs, axis=1)
    s = _dot(q, jnp.concatenate(ks, axis=1)) + jnp.concatenate(bias, axis=1)
    update(s, lambda pb: _dot_nt(pb, v_all))

    @pl.when(p == n_steps - 1)
    def _():
        acc = acc_scr[...] / l_scr[...]
        o_ref[...] = jnp.concatenate(
            [acc[h * n_tok:(h + 1) * n_tok, h * DH_A:(h + 1) * DH_A] for h in range(H_A)], axis=0).astype(o_ref.dtype)


def _fox_sample(l, page_flat, q_hq, cache_kt, cache_vt, cpool, k_new, v_new, lf_new, tri_u, n_pages):
    n_seq, nq, _ = q_hq.shape
    page = cache_kt.shape[-1]
    g_pages = max(g for g in range(1, PAGES_PER_STEP + 1) if n_pages % g == 0)
    n_steps = n_pages // g_pages
    slot = lambda s, p, g: s * n_pages + (n_pages - 1 - (p * g_pages + g))
    idx = lambda g: (lambda s, p, pt: (l, pt[slot(s, p, g)], 0, 0, 0))
    cidx = lambda g: (lambda s, p, pt: (l, pt[slot(s, p, g)], 0, 0))
    kv_specs = [pl.BlockSpec((None, None, H_A, DH_A, page), idx(g)) for g in range(g_pages)]
    c_specs = [pl.BlockSpec((None, None, H_A, page), cidx(g)) for g in range(g_pages)]
    seq_spec = lambda a, b: pl.BlockSpec((None, a, b), lambda s, p, pt: (s, 0, 0))
    grid_spec = pltpu.PrefetchScalarGridSpec(
        num_scalar_prefetch=1,
        grid=(n_seq, n_steps),
        in_specs=[seq_spec(nq, D_A)] + kv_specs + kv_specs + c_specs
                 + [seq_spec(nq, D_A), seq_spec(nq, D_A), seq_spec(1, LANES),
                    pl.BlockSpec((LANES, LANES), lambda s, p, pt: (0, 0))],
        out_specs=seq_spec(nq, DH_A),
        scratch_shapes=[pltpu.VMEM((nq, 1), F32), pltpu.VMEM((nq, 1), F32),
                        pltpu.VMEM((nq, D_A), F32), pltpu.VMEM((H_A, 1), F32)],
    )
    return pl.pallas_call(
        functools.partial(_fox_sample_kernel, n_steps=n_steps, g_pages=g_pages, n_tok=nq // H_A),
        grid_spec=grid_spec,
        out_shape=jax.ShapeDtypeStruct((n_seq, nq, DH_A), BF16),
        compiler_params=_cparams(2),
        name="fox_sample",
    )(page_flat, q_hq, *([cache_kt] * g_pages), *([cache_vt] * g_pages), *([cpool] * g_pages),
      k_new, v_new, lf_new, tri_u)


CONV_PAD = 32


def _conv_kernel(u_ref, buf_ref, w_ref, b_ref, g_ref, beta_ref, act_ref, nbuf_ref, ext, dw, *, nb, seq, tr):
    n_tiles = -(-seq // tr)
    keep = CONV_W - 1
    for s in range(nb):
        ext[s, 0:CONV_PAD - keep, :] = jnp.zeros((CONV_PAD - keep, C_B), F32)
        ext[s, CONV_PAD - keep:CONV_PAD, :] = buf_ref[s]
        ext[s, CONV_PAD:CONV_PAD + seq, :] = u_ref[s]
        if n_tiles * tr > seq:
            ext[s, CONV_PAD + seq:CONV_PAD + n_tiles * tr, :] = jnp.zeros((n_tiles * tr - seq, C_B), F32)
        nbuf_ref[s] = ext[s, CONV_PAD + seq - keep:CONV_PAD + seq, :]

        def tile(i, _, s=s):
            r0 = pl.multiple_of(i * tr, SUBLANES)
            for c in range(C_B // LANES):
                win = ext[s, pl.ds(r0, tr + CONV_PAD), c * LANES:(c + 1) * LANES]
                shifted = [win] + [pltpu.roll(win, tr + CONV_PAD - sh, 0) for sh in range(1, SUBLANES)]
                acc = jnp.broadcast_to(b_ref[:, c * LANES:(c + 1) * LANES], (tr, LANES))
                for j in range(CONV_W):
                    a, sh = divmod(CONV_PAD - keep + j, SUBLANES)
                    acc = acc + shifted[sh][a * SUBLANES:a * SUBLANES + tr, :] * w_ref[j:j + 1, c * LANES:(c + 1) * LANES]
                dw[pl.ds(r0, tr), c * LANES:(c + 1) * LANES] = acc
            return 0

        lax.fori_loop(0, n_tiles, tile, 0)

        def norm(r0, n, s=s):
            x = dw[pl.ds(r0, n), :]
            mu = jnp.mean(x, axis=-1, keepdims=True)
            xc = x - mu
            var = jnp.mean(xc * xc, axis=-1, keepdims=True)
            y = xc * lax.rsqrt(var + EPS) * g_ref[...] + beta_ref[...]
            act_ref[s, pl.ds(r0, n), :] = (y * _sigmoid(y)).astype(act_ref.dtype)

        n_full = seq // tr

        def norm_tile(i, _):
            norm(pl.multiple_of(i * tr, SUBLANES), tr)
            return 0

        lax.fori_loop(0, n_full, norm_tile, 0)
        if seq > n_full * tr:
            norm(n_full * tr, seq - n_full * tr)


def _conv(u, buf, l, w_dw, b_dw, ln_g, ln_b, nb):
    bsz, seq, _ = u.shape
    tr = LANES if seq >= LANES else seq
    n_tiles = -(-seq // tr)
    keep = CONV_W - 1
    par = lambda r: pl.BlockSpec((None, r, C_B), lambda i: (l, 0, 0))
    if buf.ndim == 4:
        buf_spec = pl.BlockSpec((None, nb, keep, C_B), lambda i: (l, i, 0, 0))
    else:
        buf_spec = pl.BlockSpec((nb, keep, C_B), lambda i: (i, 0, 0))
    return pl.pallas_call(
        functools.partial(_conv_kernel, nb=nb, seq=seq, tr=tr),
        grid=(bsz // nb,),
        in_specs=[pl.BlockSpec((nb, seq, C_B), lambda i: (i, 0, 0)), buf_spec,
                  par(CONV_PAD), par(1), par(1), par(1)],
        out_specs=[pl.BlockSpec((nb, seq, C_B), lambda i: (i, 0, 0)),
                   pl.BlockSpec((nb, keep, C_B), lambda i: (i, 0, 0))],
        out_shape=[jax.ShapeDtypeStruct((bsz, seq, C_B), BF16),
                   jax.ShapeDtypeStruct((bsz, keep, C_B), F32)],
        scratch_shapes=[pltpu.VMEM((nb, CONV_PAD + n_tiles * tr, C_B), F32),
                        pltpu.VMEM((n_tiles * tr, C_B), F32)],
        compiler_params=_cparams(1),
        name="conv",
    )(u, buf, w_dw, b_dw, ln_g, ln_b)


def _ret_chunk(s, q, k, v, n, lg):
    ii = lax.broadcasted_iota(jnp.int32, (n, n), 0)
    jj = lax.broadcasted_iota(jnp.int32, (n, n), 1)
    rel = (ii - jj).astype(F32)
    decay = jnp.where(rel >= 0, jnp.exp(jnp.maximum(rel, 0.0) * lg), 0.0)
    scores = _dot_nt(q, k.astype(BF16)) * decay
    o = _dot(scores.astype(BF16), v)
    jc = lax.broadcasted_iota(jnp.int32, (n, 1), 0).astype(F32)
    o = o + _dot(q, s.astype(BF16)) * jnp.exp((jc + 1.0) * lg)
    ks = (k * jnp.exp((n - 1.0 - jc) * lg)).astype(BF16)
    kv = lax.dot_general(ks, v, (((0,), (0,)), ((), ())), preferred_element_type=F32)
    gamma_n = float(np.exp(n * lg)) if isinstance(lg, float) else jnp.exp(n * lg)
    return gamma_n * s + kv, o


def _group_norm(o, g):
    mu = jnp.mean(o, axis=-1, keepdims=True)
    oc = o - mu
    var = jnp.mean(oc * oc, axis=-1, keepdims=True)
    return oc * lax.rsqrt(var + EPS) * g


def _head_log_gamma(h):
    hv = jnp.full((1, 1), h, jnp.int32)
    lg = jnp.full((1, 1), _LOG_GAMMA[H_C - 1], F32)
    for i in range(H_C - 2, -1, -1):
        lg = jnp.where(hv == i, jnp.float32(_LOG_GAMMA[i]), lg)
    return lg


def _ret_prompt_kernel(q_ref, k_ref, v_ref, g_ref, *rest, seq):
    o_ref, s_ref = rest[-2:]

    def chunk(r0, n, states):
        qs = [q_ref[pl.ds(r0, n), h * DK_C:(h + 1) * DK_C] for h in range(H_C)]
        ks = [k_ref[pl.ds(r0, n), h * DK_C:(h + 1) * DK_C] for h in range(H_C)]
        vs = [v_ref[pl.ds(r0, n), h * DV_C:(h + 1) * DV_C] for h in range(H_C)]
        new, outs = [], []
        for h in range(H_C):
            s, o = _ret_chunk(states[h], qs[h], ks[h], vs[h], n, _LOG_GAMMA[h])
            new.append(s)
            outs.append(_group_norm(o, g_ref[:, h * DV_C:(h + 1) * DV_C]))
        o_ref[pl.ds(r0, n), :] = jnp.concatenate(outs, axis=1).astype(o_ref.dtype)
        return tuple(new)

    states = chunk(0, N_META, tuple(jnp.zeros((DK_C, DV_C), F32) for _ in range(H_C)))
    n_chunks = (seq - N_META) // RET_BLK
    states = lax.fori_loop(0, n_chunks,
                           lambda i, st: chunk(pl.multiple_of(N_META + i * RET_BLK, N_META), RET_BLK, st), states,
                           unroll=2 if n_chunks % 2 == 0 else 1)
    for h in range(H_C):
        s_ref[h] = states[h]


def _ret_prompt(qr, kr, vc, l, gn_g, s_prev):
    b, seq, _ = qr.shape
    depth = gn_g.shape[0]
    assert (seq - N_META) % RET_BLK == 0
    full = lambda n: pl.BlockSpec((None, seq, n), lambda i: (i, 0, 0))
    s_spec, s_shape, extra = _stacked_out((depth, b, H_C, DK_C, DV_C), (None, H_C, DK_C, DV_C),
                                          lambda i: (i, 0, 0, 0), l, s_prev)
    return pl.pallas_call(
        functools.partial(_ret_prompt_kernel, seq=seq),
        grid=(b,),
        in_specs=[full(D_QK_C), full(D_QK_C), full(D_V_C), pl.BlockSpec((None, 1, D_V_C), lambda i: (l, 0, 0))]
                 + [s for s, _ in extra],
        out_specs=[full(D_V_C), s_spec],
        out_shape=[jax.ShapeDtypeStruct((b, seq, D_V_C), BF16), s_shape],
        input_output_aliases={4: 1} if extra else {},
        compiler_params=_cparams(1),
        name="ret_prompt",
    )(qr, kr, vc, gn_g, *[a for _, a in extra])


def _ret_sample_kernel(q_ref, k_ref, v_ref, g_ref, s0_ref, *rest, nb, n_tok):
    o_ref, s_ref = rest[-2:]
    lg = _head_log_gamma(pl.program_id(1))
    g = g_ref[...]
    for s in range(nb):
        s_new, o = _ret_chunk(s0_ref[s], q_ref[s], k_ref[s], v_ref[s], n_tok, lg)
        o_ref[s] = _group_norm(o, g).astype(o_ref.dtype)
        s_ref[s] = s_new


def _ret_sample(qr, kr, vc, l, gn_g, state, nb, s_prev):
    n_seq, n_tok, _ = qr.shape
    qk = pl.BlockSpec((nb, n_tok, DK_C), lambda i, h: (i, 0, h))
    vv = pl.BlockSpec((nb, n_tok, DV_C), lambda i, h: (i, 0, h))
    s_spec, s_shape, extra = _stacked_out(state.shape, (nb, None, DK_C, DV_C), lambda i, h: (i, h, 0, 0), l, s_prev)
    return pl.pallas_call(
        functools.partial(_ret_sample_kernel, nb=nb, n_tok=n_tok),
        grid=(n_seq // nb, H_C),
        in_specs=[qk, qk, vv, pl.BlockSpec((None, 1, DV_C), lambda i, h: (l, 0, h)),
                  pl.BlockSpec((None, nb, None, DK_C, DV_C), lambda i, h: (l, i, h, 0, 0))] + [s for s, _ in extra],
        out_specs=[vv, s_spec],
        out_shape=[jax.ShapeDtypeStruct((n_seq, n_tok, D_V_C), BF16), s_shape],
        input_output_aliases={5: 1} if extra else {},
        compiler_params=_cparams(2),
        name="ret_sample",
    )(qr, kr, vc, gn_g, state, *[a for _, a in extra])


def _merge_kernel(x_ref, oa_ref, cb_ref, on_ref, gpre_ref, gpost_ref, wgc_ref, wgate_ref, wao_ref,
                  wpw_ref, wro_ref, wo_ref, y_ref):
    x = x_ref[...]
    h = _rms(x, gpre_ref[...]).astype(BF16)
    zc = _dot(h, wgc_ref[...])
    y_c = _dot((zc * _sigmoid(zc) * on_ref[...]).astype(BF16), wro_ref[...])
    y_a = _dot(oa_ref[...], wao_ref[...])
    y_b = _dot(cb_ref[...], wpw_ref[...])
    merged = (_sigmoid(_dot(h, wgate_ref[:, 0:D_MODEL])) * y_a
              + _sigmoid(_dot(h, wgate_ref[:, D_MODEL:2 * D_MODEL])) * y_b
              + _sigmoid(_dot(h, wgate_ref[:, 2 * D_MODEL:])) * y_c)
    out = _dot(merged.astype(BF16), wo_ref[...])
    y_ref[...] = x + _rms(out, gpost_ref[...])


def _merge(x, oa, cb, on, l, gpre, gpost, wgc, wgate, wao, wpw, wro, wo, tm):
    t = x.shape[0]
    row = lambda n: pl.BlockSpec((tm, n), lambda i: (i, 0))
    w = lambda k, n: _resident((None, k, n), lambda i: (l, 0, 0))
    return pl.pallas_call(
        _merge_kernel,
        grid=(t // tm,),
        in_specs=[row(D_MODEL), row(D_A), row(C_B), row(D_V_C), w(1, D_MODEL), w(1, D_MODEL),
                  w(D_MODEL, D_V_C), w(D_MODEL, N_BRANCH * D_MODEL), w(D_A, D_MODEL), w(C_B, D_MODEL),
                  w(D_V_C, D_MODEL), w(D_MODEL, D_MODEL)],
        out_specs=row(D_MODEL),
        out_shape=jax.ShapeDtypeStruct((t, D_MODEL), F32),
        compiler_params=_cparams(1),
        name="merge",
    )(x, oa, cb, on, gpre, gpost, wgc, wgate, wao, wpw, wro, wo)


FF_CHUNK = 1024


def _ffn_kernel(x_ref, gpre_ref, gpost_ref, wup_ref, wdn_ref, y_ref):
    x = x_ref[...]
    h = _rms(x, gpre_ref[...]).astype(BF16)
    ff = jnp.zeros(x.shape, F32)
    for c in range(D_FF // FF_CHUNK):
        up = jnp.maximum(_dot(h, wup_ref[:, c * FF_CHUNK:(c + 1) * FF_CHUNK]), 0.0)
        ff = ff + _dot((up * up).astype(BF16), wdn_ref[c * FF_CHUNK:(c + 1) * FF_CHUNK, :])
    y_ref[...] = x + _rms(ff, gpost_ref[...])


def _ffn(x, l, gpre, gpost, wup, wdn, tm):
    t = x.shape[0]
    row = pl.BlockSpec((tm, D_MODEL), lambda i: (i, 0))
    w = lambda k, n: _resident((None, k, n), lambda i: (l, 0, 0))
    return pl.pallas_call(
        _ffn_kernel,
        grid=(t // tm,),
        in_specs=[row, w(1, D_MODEL), w(1, D_MODEL), w(D_MODEL, D_FF), w(D_FF, D_MODEL)],
        out_specs=row,
        out_shape=jax.ShapeDtypeStruct((t, D_MODEL), F32),
        compiler_params=_cparams(1),
        name="ffn",
    )(x, gpre, gpost, wup, wdn)


def _rope_tables(pos):
    half = DK_C // 2
    inv = ROPE_BASE ** (-jnp.arange(half, dtype=F32) / half)
    ang = pos.astype(F32)[:, None] * inv[None, :]
    cos, sin = jnp.cos(ang), jnp.sin(ang)
    return jnp.concatenate([cos, cos], axis=1), jnp.concatenate([-sin, sin], axis=1)


def kernel(x_prompt, x_sample, cache_k, cache_v, cache_logf, state_conv, state_ret, page_table,
           meta_tokens, norm_mix_pre, norm_mix_post, norm_ffn_pre, norm_ffn_post,
           w_in, b_forget, w_dw, b_dw, ln_conv_g, ln_conv_b, w_pw_out, w_attn_out,
           gn_ret_g, w_ret_out, w_o, w_ff_up, w_ff_down):
    depth = w_in.shape[0]
    bp, seq_p, _ = x_prompt.shape
    n_dec, n_tok, _ = x_sample.shape
    n_pages = page_table.shape[1]
    n_pool, page = cache_k.shape[1], cache_k.shape[2]
    assert page == LANES
    past_len = n_pages * page
    lp = N_META + seq_p
    tp, ts = bp * lp, n_dec * n_tok

    cuts = np.cumsum([0, D_A, D_A, D_A, H_A, 2 * C_B, D_QK_C, D_QK_C, D_V_C, D_V_C, N_BRANCH * D_MODEL])
    seg = lambda a, b: w_in[:, :, int(cuts[a]):int(cuts[b])].astype(BF16)
    wa, wglu, wqk, wv, wgc, wgate = seg(0, 3), seg(4, 5), seg(5, 7), seg(7, 8), seg(8, 9), seg(9, 10)
    wf = jnp.pad(w_in[:, :, int(cuts[3]):int(cuts[4])], ((0, 0), (0, 0), (0, LANES - H_A))).astype(BF16)
    bfp = jnp.pad(b_forget, ((0, 0), (0, LANES - H_A)))[:, None, :]
    wao, wpw, wro, wo = (w.astype(BF16) for w in (w_attn_out, w_pw_out, w_ret_out, w_o))
    wup, wdn = w_ff_up.astype(BF16), w_ff_down.astype(BF16)
    vec = lambda a: a[:, None, :]
    g_mpre, g_mpost, g_fpre, g_fpost = vec(norm_mix_pre), vec(norm_mix_post), vec(norm_ffn_pre), vec(norm_ffn_post)
    wdw = jnp.pad(w_dw, ((0, 0), (0, CONV_PAD - CONV_W), (0, 0)))
    bdw, lng, lnb, gng = vec(b_dw), vec(ln_conv_g), vec(ln_conv_b), vec(gn_ret_g)

    tm_in_p = _row_tile(tp, 700, must_divide=lp)
    tm_in_s = _row_tile(ts, 256)
    tm_p = _row_tile(tp, 400)
    tm_ffn_p = _row_tile(tp, 700)
    tm_s = _row_tile(ts, 512)
    cos_p, sin_p = _rope_tables(jnp.arange(lp))
    cos_s, sin_s = _rope_tables(past_len + jnp.arange(n_tok))
    cos_s, sin_s = (jnp.tile(a, (tm_in_s // n_tok, 1)) for a in (cos_s, sin_s))
    sel = _fox_aug_select()

    tri_u = jnp.asarray(np.triu(np.ones((LANES, LANES), np.float32)), dtype=BF16)
    cache_kt = cache_k.transpose(0, 1, 3, 4, 2)
    cache_vt = cache_v.transpose(0, 1, 3, 4, 2)
    lf_t = cache_logf.transpose(0, 1, 3, 2).reshape(depth * n_pool * H_A, page)
    cpool = _lane_cumsum(lf_t, tri_u).reshape(depth, n_pool, H_A, page)
    page_flat = page_table.reshape(-1)

    xp = jnp.concatenate([jnp.broadcast_to(meta_tokens[None], (bp, N_META, D_MODEL)), x_prompt], axis=1)
    xp = xp.reshape(tp, D_MODEL)
    xs = x_sample.reshape(ts, D_MODEL)
    buf0 = jnp.zeros((bp, CONV_W - 1, C_B), F32)
    nq = H_A * n_tok
    assert nq <= LANES
    own_head = jnp.asarray(np.arange(nq)[:, None] // n_tok == np.arange(D_A)[None, :] // DH_A)
    lane_grp = np.arange(LANES)
    t_grp = jnp.asarray(((lane_grp[:, None] // n_tok == lane_grp[None, :] // n_tok)
                         & (lane_grp[:, None] <= lane_grp[None, :])).astype(np.float32), dtype=BF16)

    def head_major(a):
        a = jnp.broadcast_to(a.reshape(n_dec, 1, n_tok, D_A), (n_dec, H_A, n_tok, D_A)).reshape(n_dec, nq, D_A)
        return jnp.where(own_head, a, 0).astype(BF16)

    p_lf, p_conv, s_lf, s_conv = [], [], [], []
    pk = pv = p_ret = sk = sv = s_ret = None
    for l in range(depth):
        qa, pk, pv, lf, u, qr, kr, vc = _inproj(xp, l, g_mpre, wa, wf, bfp, wglu, wqk, wv, cos_p, sin_p,
                                                tm_in_p, pk, pv)
        r3 = lambda a: a.reshape(bp, lp, a.shape[-1])
        oa = _fox_prompt(r3(qa), pk.reshape(depth, bp, lp, D_A), pv.reshape(depth, bp, lp, D_A), l, r3(lf), sel)
        cact, nbuf = _conv(r3(u), buf0, l, wdw, bdw, lng, lnb, 1)
        on, p_ret = _ret_prompt(r3(qr), r3(kr), r3(vc), l, gng, p_ret)
        xp = _merge(xp, oa.reshape(tp, D_A), cact.reshape(tp, C_B), on.reshape(tp, D_V_C), l,
                    g_mpre, g_mpost, wgc, wgate, wao, wpw, wro, wo, tm_p)
        xp = _ffn(xp, l, g_fpre, g_fpost, wup, wdn, tm_ffn_p)
        p_lf.append(lf[:, :H_A].reshape(bp, lp, H_A))
        p_conv.append(nbuf)

        qa, sk, sv, lf, u, qr, kr, vc = _inproj(xs, l, g_mpre, wa, wf, bfp, wglu, wqk, wv, cos_s, sin_s,
                                                tm_in_s, sk, sv)
        s3 = lambda a: a.reshape(n_dec, n_tok, a.shape[-1])
        lf8 = lf[:, :H_A]
        lf_new = jnp.pad(lf8.reshape(n_dec, n_tok, H_A).transpose(0, 2, 1).reshape(n_dec, 1, nq),
                         ((0, 0), (0, 0), (0, LANES - nq)))
        o_hq = _fox_sample(l, page_flat, head_major(qa), cache_kt, cache_vt, cpool,
                           head_major(sk[l]), head_major(sv[l]), lf_new, t_grp, n_pages)
        oa = o_hq.reshape(n_dec, H_A, n_tok, DH_A).transpose(0, 2, 1, 3).reshape(ts, D_A)
        cact, nbuf = _conv(s3(u), state_conv, l, wdw, bdw, lng, lnb, 16)
        on, s_ret = _ret_sample(s3(qr), s3(kr), s3(vc), l, gng, state_ret, 16, s_ret)
        xs = _merge(xs, oa, cact.reshape(ts, C_B), on.reshape(ts, D_V_C), l,
                    g_mpre, g_mpost, wgc, wgate, wao, wpw, wro, wo, tm_s)
        xs = _ffn(xs, l, g_fpre, g_fpost, wup, wdn, tm_s)
        s_lf.append(lf8.reshape(n_dec, n_tok, H_A))
        s_conv.append(nbuf)

    y_prompt = xp.reshape(bp, lp, D_MODEL)[:, N_META:]
    y_sample = xs.reshape(n_dec, n_tok, D_MODEL)
    return (y_prompt, y_sample,
            pk.reshape(depth, bp, lp, H_A, DH_A), pv.reshape(depth, bp, lp, H_A, DH_A),
            jnp.stack(p_lf), jnp.stack(p_conv), p_ret,
            sk.reshape(depth, n_dec, n_tok, H_A, DH_A), sv.reshape(depth, n_dec, n_tok, H_A, DH_A),
            jnp.stack(s_lf), jnp.stack(s_conv), s_ret)
```
